```python
import jax
import jax.numpy as jnp
from jax import lax
import numpy as np

D_MODEL = 2048
BATCH = 2
SEQ = 4096
DEPTH = 2

GRID_W = 64
CTX_LEN = 256
N_MIXERS = 2
MIXER_DELTANET = 0
MIXER_FOURIER = 1
N_DELTANET_LAYERS = (DEPTH + 1) // 2
N_FOURIER_LAYERS = DEPTH // 2

D_FF = 5632
FFN_RES = 0.5
N_SUB = 3
N_MOD = 3 * N_SUB

HK = 16
HV = 32
DK = 128
DV = 128
KEY_DIM = HK * DK
VAL_DIM = HV * DV
QKV_DIM = 2 * KEY_DIM + VAL_DIM
DN_IN_DIM = QKV_DIM + VAL_DIM + 4 * HV
CONV_W = 5
CONV_PAD = CONV_W // 2
CHUNK = 64

N_FGROUPS = 4
FGROUP_DIM = D_MODEL // N_FGROUPS

EPS = 1e-6
MOD_INIT = 0.5

kernel_name = "interleaved_deltanet_fourier_macaron_dit"


def rms_norm(x, g):
    xf = x.astype(jnp.float32)
    y = xf * lax.rsqrt(jnp.mean(xf * xf, axis=-1, keepdims=True) + EPS)
    return (y * g.astype(jnp.float32)).astype(x.dtype)


def ada_norm(x, g, shift, scale):
    return rms_norm(x, g) * (1 + scale) + shift


def modulation(cond, w_mod, b_mod):
    m = jax.nn.silu(cond) @ w_mod + b_mod
    return jnp.split(m[..., None, :], N_MOD, axis=-1)


def swiglu(h, w_gate, w_up, w_down):
    return (jax.nn.silu(h @ w_gate) * (h @ w_up)) @ w_down


def ffn_half(x, g, mod3, w):
    shift, scale, gate = mod3
    return x + FFN_RES * gate * swiglu(ada_norm(x, g, shift, scale), *w)


def centred_dwconv(x, w):
    ch = x.shape[-1]
    return lax.conv_general_dilated(
        x, w[:, None, :].astype(x.dtype), window_strides=(1,),
        padding=[(CONV_PAD, CONV_PAD)], dimension_numbers=("NWC", "WIO", "NWC"),
        feature_group_count=ch)


def l2_normalize(t):
    tf = t.astype(jnp.float32)
    return tf * lax.rsqrt(jnp.sum(tf * tf, axis=-1, keepdims=True) + EPS)


def chunk_gated_delta(q, k, v, g, beta, s0):
    b, seq_len, h = q.shape[:3]
    n = seq_len // CHUNK

    def to_chunks(t):
        t = t.astype(jnp.float32).reshape(b, n, CHUNK, h, *t.shape[3:])
        return jnp.moveaxis(t, 3, 1)

    q = to_chunks(q) * (DK ** -0.5)
    k, v, g, beta = to_chunks(k), to_chunks(v), to_chunks(g), to_chunks(beta)
    g = jnp.cumsum(g, axis=-1)
    lower = jnp.tril(jnp.ones((CHUNK, CHUNK), dtype=bool))
    diff = g[..., :, None] - g[..., None, :]
    decay = jnp.exp(jnp.where(lower, diff, -jnp.inf))
    k_beta = k * beta[..., None]
    v_beta = v * beta[..., None]
    lmat = jnp.einsum("bhnid,bhnjd->bhnij", k_beta, k) * decay
    rhs = jnp.concatenate([v_beta, k_beta * jnp.exp(g)[..., None]], axis=-1)
    sol = lax.linalg.triangular_solve(lmat, rhs, left_side=True, lower=True,
                                      unit_diagonal=True)
    u, w = sol[..., :DV], sol[..., DV:]
    attn = jnp.einsum("bhnid,bhnjd->bhnij", q, k) * decay
    q_dec = q * jnp.exp(g)[..., None]
    k_tail = k * jnp.exp(g[..., -1:] - g)[..., None]
    g_last = jnp.exp(g[..., -1])

    xs = tuple(jnp.moveaxis(t, 2, 0) for t in (u, w, q_dec, k_tail, attn, g_last))

    def step(state, inp):
        u_n, w_n, qd_n, kt_n, a_n, gl_n = inp
        v_new = u_n - jnp.einsum("bhcd,bhde->bhce", w_n, state)
        o_n = (jnp.einsum("bhcd,bhde->bhce", qd_n, state)
               + jnp.einsum("bhij,bhje->bhie", a_n, v_new))
        state = state * gl_n[..., None, None] + jnp.einsum("bhcd,bhce->bhde", kt_n, v_new)
        return state, o_n

    s_fin, o = lax.scan(step, s0.astype(jnp.float32), xs)
    o = jnp.transpose(o, (1, 0, 3, 2, 4)).reshape(b, seq_len, h, DV)
    return o, s_fin


def deltanet_inputs(h, w_in, conv_w, a_log, dt_bias):
    b, seq_len, _ = h.shape
    p = h @ w_in
    qkv, z, bb, aa = jnp.split(p, [QKV_DIM, QKV_DIM + VAL_DIM, QKV_DIM + VAL_DIM + 2 * HV], axis=-1)
    qkv = jax.nn.silu(centred_dwconv(qkv, conv_w))
    q, k, v = jnp.split(qkv, [KEY_DIM, 2 * KEY_DIM], axis=-1)
    q = jnp.repeat(l2_normalize(q.reshape(b, seq_len, HK, DK)), HV // HK, axis=2)
    k = jnp.repeat(l2_normalize(k.reshape(b, seq_len, HK, DK)), HV // HK, axis=2)
    v = v.reshape(b, seq_len, HV, DV)
    z = z.reshape(b, seq_len, HV, DV)
    beta = jax.nn.sigmoid(bb.reshape(b, seq_len, 2, HV).astype(jnp.float32))
    g = -jnp.exp(a_log.astype(jnp.float32)) * jax.nn.softplus(
        aa.reshape(b, seq_len, 2, HV).astype(jnp.float32) + dt_bias.astype(jnp.float32))
    return q, k, v, z, g, beta


def gated_out(o, z, norm_g, w_out):
    b, seq_len = o.shape[:2]
    y = rms_norm(o, norm_g) * jax.nn.silu(z.astype(jnp.float32))
    return y.reshape(b, seq_len, VAL_DIM).astype(z.dtype) @ w_out


def deltanet_mixer(u, uc, w_in, conv_w, a_log, dt_bias, norm_g, w_out, ctx_out):
    q, k, v, z, g, beta = deltanet_inputs(u, w_in, conv_w, a_log, dt_bias)
    qc, kc, vc, zc, gc, betac = deltanet_inputs(uc, w_in, conv_w, a_log, dt_bias)
    s_zero = jnp.zeros((u.shape[0], HV, DK, DV), jnp.float32)
    o_lat, o_ctx = [], []
    for d in range(2):
        rev = (lambda t: jnp.flip(t, axis=1)) if d == 1 else (lambda t: t)
        oc, s_ctx = chunk_gated_delta(rev(qc), rev(kc), rev(vc), rev(gc[:, :, d]),
                                      rev(betac[:, :, d]), s_zero)
        ol, _ = chunk_gated_delta(rev(q), rev(k), rev(v), rev(g[:, :, d]),
                                  rev(beta[:, :, d]), s_ctx)
        o_lat.append(rev(ol))
        o_ctx.append(rev(oc))
    y = gated_out(o_lat[0] + o_lat[1], z, norm_g, w_out)
    yc = gated_out(o_ctx[0] + o_ctx[1], zc, norm_g, w_out) if ctx_out else None
    return y, yc


def fourier_mixer(u, w_out):
    b, seq_len, d = u.shape
    ug = u.astype(jnp.float32).reshape(b, seq_len, N_FGROUPS, FGROUP_DIM)
    y = jnp.fft.fft2(ug, axes=(1, 3), norm="ortho").real
    return y.reshape(b, seq_len, d).astype(u.dtype) @ w_out


def setup_inputs(seed: int = 0) -> dict:
    key = jax.random.key(seed)
    ks = jax.random.split(key, 24)
    f32 = jnp.float32

    def dense(k, shape, fan_in):
        return jax.random.normal(k, shape, f32) * (fan_in ** -0.5)

    x = jax.random.normal(ks[0], (BATCH, SEQ, D_MODEL), f32)
    c = jax.random.normal(ks[1], (BATCH, D_MODEL), f32)
    ctx = jax.random.normal(ks[2], (BATCH, CTX_LEN, D_MODEL), f32)
    c_ctx = jax.random.normal(ks[3], (D_MODEL,), f32)
    norm_g = 1.0 + 0.02 * jax.random.normal(ks[4], (DEPTH, N_SUB, D_MODEL), f32)
    mod_w = MOD_INIT * dense(ks[5], (DEPTH, D_MODEL, N_MOD * D_MODEL), D_MODEL)
    mod_b = 0.02 * jax.random.normal(ks[6], (DEPTH, N_MOD * D_MODEL), f32)
    ffn_w_gate = dense(ks[7], (DEPTH, 2, D_MODEL, D_FF), D_MODEL)
    ffn_w_up = dense(ks[8], (DEPTH, 2, D_MODEL, D_FF), D_MODEL)
    ffn_w_down = dense(ks[9], (DEPTH, 2, D_FF, D_MODEL), D_FF)
    dn_w_in = dense(ks[10], (N_DELTANET_LAYERS, D_MODEL, DN_IN_DIM), D_MODEL)
    dn_conv_w = dense(ks[11], (N_DELTANET_LAYERS, CONV_W, QKV_DIM), CONV_W)
    a_init = jax.random.uniform(ks[12], (N_DELTANET_LAYERS, 2, HV), f32, 1.0, 16.0)
    dn_a_log = jnp.log(a_init)
    dt = jnp.exp(jax.random.uniform(ks[13], (N_DELTANET_LAYERS, 2, HV), f32,
                                    jnp.log(0.001), jnp.log(0.1)))
    dn_dt_bias = dt + jnp.log(-jnp.expm1(-dt))
    dn_norm_g = 1.0 + 0.02 * jax.random.normal(ks[14], (N_DELTANET_LAYERS, DV), f32)
    dn_w_out = dense(ks[15], (N_DELTANET_LAYERS, VAL_DIM, D_MODEL), VAL_DIM)
    fn_w_out = dense(ks[16], (N_FOURIER_LAYERS, D_MODEL, D_MODEL), D_MODEL)
    final_norm_g = 1.0 + 0.02 * jax.random.normal(ks[17], (D_MODEL,), f32)
    return {"x": x, "c": c, "ctx": ctx, "c_ctx": c_ctx, "norm_g": norm_g,
            "mod_w": mod_w, "mod_b": mod_b, "ffn_w_gate": ffn_w_gate,
            "ffn_w_up": ffn_w_up, "ffn_w_down": ffn_w_down, "dn_w_in": dn_w_in,
            "dn_conv_w": dn_conv_w, "dn_a_log": dn_a_log, "dn_dt_bias": dn_dt_bias,
            "dn_norm_g": dn_norm_g, "dn_w_out": dn_w_out, "fn_w_out": fn_w_out,
            "final_norm_g": final_norm_g}


def reference(x, c, ctx, c_ctx, norm_g, mod_w, mod_b, ffn_w_gate, ffn_w_up, ffn_w_down,
              dn_w_in, dn_conv_w, dn_a_log, dn_dt_bias, dn_norm_g, dn_w_out, fn_w_out,
              final_norm_g):
    h, hc = x, ctx
    for i in range(DEPTH):
        kind = i % N_MIXERS
        j = i // N_MIXERS
        ctx_next = any(l % N_MIXERS == MIXER_DELTANET for l in range(i + 1, DEPTH))
        ctx_here = ctx_next or kind == MIXER_DELTANET
        m = modulation(c, mod_w[i], mod_b[i])
        mc = modulation(c_ctx, mod_w[i], mod_b[i])
        ffn_a = (ffn_w_gate[i, 0], ffn_w_up[i, 0], ffn_w_down[i, 0])
        ffn_b = (ffn_w_gate[i, 1], ffn_w_up[i, 1], ffn_w_down[i, 1])

        h = ffn_half(h, norm_g[i, 0], m[0:3], ffn_a)
        if ctx_here:
            hc = ffn_half(hc, norm_g[i, 0], mc[0:3], ffn_a)

        u = ada_norm(h, norm_g[i, 1], m[3], m[4])
        if kind == MIXER_DELTANET:
            uc = ada_norm(hc, norm_g[i, 1], mc[3], mc[4])
            y, yc = deltanet_mixer(u, uc, dn_w_in[j], dn_conv_w[j], dn_a_log[j], dn_dt_bias[j],
                                   dn_norm_g[j], dn_w_out[j], ctx_next)
        else:
            y = fourier_mixer(u, fn_w_out[j])
            yc = (fourier_mixer(ada_norm(hc, norm_g[i, 1], mc[3], mc[4]), fn_w_out[j])
                  if ctx_next else None)
        h = h + m[5] * y
        if ctx_next:
            hc = hc + mc[5] * yc
            hc = ffn_half(hc, norm_g[i, 2], mc[6:9], ffn_b)

        h = ffn_half(h, norm_g[i, 2], m[6:9], ffn_b)
    return rms_norm(h, final_norm_g)
```

```python
import functools
import math

import jax
import jax.numpy as jnp
from jax import lax
from jax.experimental import pallas as pl
from jax.experimental.pallas import tpu as pltpu

F32 = jnp.float32
BF16 = jnp.bfloat16

EPS = 1e-6
FFN_RES = 0.5
N_SUB = 3
HK = 16
HV = 32
DK = 128
DV = 128
CONV_W = 5
N_FGROUPS = 4
DELTA_CHUNK = 64

LANES = 128
MIB = 1024 * 1024


def _cparams(semantics, vmem_mib):
    return pltpu.CompilerParams(dimension_semantics=semantics,
                                vmem_limit_bytes=int(vmem_mib * MIB))


def _dot(a, b):
    return jnp.dot(a, b, preferred_element_type=F32)


def _dot_nt(a, b):
    return lax.dot_general(a, b, (((1,), (1,)), ((), ())), preferred_element_type=F32)


def _dot_tn(a, b):
    return lax.dot_general(a, b, (((0,), (0,)), ((), ())), preferred_element_type=F32)


def _silu(x):
    return x * jax.nn.sigmoid(x)


def _ada_norm(x, g, shift, scale):
    ms = jnp.mean(x * x, axis=-1, keepdims=True)
    return (x * lax.rsqrt(ms + EPS) * g) * (1.0 + scale) + shift


def _mod_kernel(cond_ref, w_ref, b_ref, o_ref):
    s = _silu(cond_ref[...]).astype(BF16)
    o_ref[0] = _dot(s, w_ref[0].astype(BF16)) + b_ref[0]


def _modulation(cond, mod_w, mod_b):
    depth, d, n = mod_w.shape
    tn = 1024
    return pl.pallas_call(
        _mod_kernel,
        out_shape=jax.ShapeDtypeStruct((depth, 8, n), F32),
        grid=(depth, n // tn),
        in_specs=[pl.BlockSpec((8, d), lambda l, j: (0, 0)),
                  pl.BlockSpec((1, d, tn), lambda l, j: (l, 0, j)),
                  pl.BlockSpec((1, 1, tn), lambda l, j: (l, 0, j))],
        out_specs=pl.BlockSpec((1, 8, tn), lambda l, j: (l, 0, j)),
        compiler_params=_cparams(("parallel", "parallel"), 40),
        name="modulation",
    )(cond, mod_w, mod_b.reshape(depth, 1, n))


def _ffn_kernel(x_ref, mod_ref, g_ref, wg_ref, wu_ref, wd_ref, fg_ref, o_ref, h_ref, *,
                final_norm):
    j = pl.program_id(1)
    nj = pl.num_programs(1)

    @pl.when(j == 0)
    def _():
        h_ref[...] = _ada_norm(x_ref[...], g_ref[...], mod_ref[0, 0:1, :],
                               mod_ref[0, 1:2, :]).astype(BF16)

    h = h_ref[...]
    a = (_silu(_dot(h, wg_ref[...])) * _dot(h, wu_ref[...])).astype(BF16)
    part = _dot(a, wd_ref[...])

    @pl.when(j == 0)
    def _():
        o_ref[...] = part

    @pl.when(j > 0)
    def _():
        o_ref[...] += part

    @pl.when(j == nj - 1)
    def _():
        r = x_ref[...] + (FFN_RES * mod_ref[0, 2:3, :]) * o_ref[...]
        if final_norm:
            ms = jnp.mean(r * r, axis=-1, keepdims=True)
            r = r * lax.rsqrt(ms + EPS) * fg_ref[...]
        o_ref[...] = r


def _ffn(x, modv, g, wg, wu, wd, final_g=None):
    t, d = x.shape
    f = wg.shape[1]
    tm = min(512, t)
    tf = 512
    tiles_per_mod = t // tm // modv.shape[0]
    fg = (final_g if final_g is not None else g).reshape(1, d)
    return pl.pallas_call(
        functools.partial(_ffn_kernel, final_norm=final_g is not None),
        out_shape=jax.ShapeDtypeStruct((t, d), F32),
        grid=(t // tm, f // tf),
        in_specs=[pl.BlockSpec((tm, d), lambda i, j: (i, 0)),
                  pl.BlockSpec((1, 8, d), lambda i, j: (i // tiles_per_mod, 0, 0)),
                  pl.BlockSpec((1, d), lambda i, j: (0, 0)),
                  pl.BlockSpec((d, tf), lambda i, j: (0, j)),
                  pl.BlockSpec((d, tf), lambda i, j: (0, j)),
                  pl.BlockSpec((tf, d), lambda i, j: (j, 0)),
                  pl.BlockSpec((1, d), lambda i, j: (0, 0))],
        out_specs=pl.BlockSpec((tm, d), lambda i, j: (i, 0)),
        scratch_shapes=[pltpu.VMEM((tm, d), BF16)],
        compiler_params=_cparams(("parallel", "arbitrary"), 48),
        name="ffn_half",
    )(x, modv, g.reshape(1, d), wg, wu, wd, fg)


def _mm_kernel(*refs, norm, resid):
    it = iter(refs)
    a_ref = next(it)
    b_ref = next(it)
    if norm:
        mod_ref, g_ref = next(it), next(it)
    elif resid:
        mod_ref = next(it)
    if resid:
        r_ref = next(it)
    o_ref = next(it)

    if norm:
        h_ref = next(it)

        @pl.when(pl.program_id(1) == 0)
        def _():
            h_ref[...] = _ada_norm(a_ref[...], g_ref[...], mod_ref[0, 0:1, :],
                                   mod_ref[0, 1:2, :]).astype(BF16)
        a = h_ref[...]
    else:
        a = a_ref[...]
    acc = _dot(a, b_ref[...])
    if resid:
        acc = r_ref[...] + mod_ref[0, 2:3, :] * acc
    o_ref[...] = acc.astype(o_ref.dtype)


def _matmul(a, b, *, out_dtype, tm, tn, modv=None, norm_g=None, resid=None, vmem_mib=48,
            name="matmul"):
    m, k = a.shape
    n = b.shape[1]
    tm = min(tm, m)
    norm = norm_g is not None
    has_res = resid is not None
    in_specs = [pl.BlockSpec((tm, k), lambda i, j: (i, 0)),
                pl.BlockSpec((k, tn), lambda i, j: (0, j))]
    args = [a, b]
    if norm or has_res:
        tiles_per_mod = m // tm // modv.shape[0]
        if norm:
            in_specs.append(pl.BlockSpec((1, 8, k), lambda i, j: (i // tiles_per_mod, 0, 0)))
            in_specs.append(pl.BlockSpec((1, k), lambda i, j: (0, 0)))
            args += [modv, norm_g.reshape(1, k)]
        else:
            in_specs.append(pl.BlockSpec((1, 8, tn), lambda i, j: (i // tiles_per_mod, 0, j)))
            args.append(modv)
    if has_res:
        in_specs.append(pl.BlockSpec((tm, tn), lambda i, j: (i, j)))
        args.append(resid)
    return pl.pallas_call(
        functools.partial(_mm_kernel, norm=norm, resid=has_res),
        out_shape=jax.ShapeDtypeStruct((m, n), out_dtype),
        grid=(m // tm, n // tn),
        in_specs=in_specs,
        out_specs=pl.BlockSpec((tm, tn), lambda i, j: (i, j)),
        scratch_shapes=[pltpu.VMEM((tm, k), BF16)] if norm else [],
        compiler_params=_cparams(("parallel", "arbitrary"), vmem_mib),
        name=name,
    )(*args)


_CONV_ROWS = 256
_CONV_HALO = 8


def _conv_kernel(x_ref, w_ref, o_ref, xp_ref, *, seq, tc, l2norm, out_scale):
    zeros = jnp.zeros((_CONV_HALO, tc), F32)
    xp_ref[0:_CONV_HALO, :] = zeros
    xp_ref[_CONV_HALO + seq:, :] = zeros
    xp_ref[_CONV_HALO:_CONV_HALO + seq, :] = x_ref[0]
    w = w_ref[...]
    rows = min(_CONV_ROWS, seq)
    win_rows = rows + 2 * _CONV_HALO

    def step(t, carry):
        base = pl.multiple_of(t * rows, 8)
        win = xp_ref[pl.ds(base, win_rows), :]
        acc = None
        for j in range(CONV_W):
            shift = (CONV_W // 2 - j) % win_rows
            rolled = win if shift == 0 else pltpu.roll(win, shift, 0)
            term = rolled[_CONV_HALO:_CONV_HALO + rows, :] * w[j:j + 1, :]
            acc = term if acc is None else acc + term
        y = _silu(acc)
        if l2norm:
            parts = []
            for hh in range(tc // LANES):
                yh = y[:, hh * LANES:(hh + 1) * LANES]
                ss = jnp.sum(yh * yh, axis=-1, keepdims=True)
                parts.append(yh * (lax.rsqrt(ss + EPS) * out_scale))
            y = parts[0] if len(parts) == 1 else jnp.concatenate(parts, axis=-1)
        o_ref[0, pl.ds(base, rows), :] = y.astype(o_ref.dtype)
        return carry

    lax.fori_loop(0, seq // rows, step, 0)


def _conv_act(p, conv_w, col0, ncols, *, l2norm, out_scale=1.0):
    b, s, _ = p.shape
    tc = 256
    off = col0 // tc
    return pl.pallas_call(
        functools.partial(_conv_kernel, seq=s, tc=tc, l2norm=l2norm, out_scale=out_scale),
        out_shape=jax.ShapeDtypeStruct((b, s, ncols), BF16),
        grid=(b, ncols // tc),
        in_specs=[pl.BlockSpec((1, s, tc), lambda i, j: (i, 0, j + off)),
                  pl.BlockSpec((CONV_W, tc), lambda i, j: (0, j + off))],
        out_specs=pl.BlockSpec((1, s, tc), lambda i, j: (i, 0, j)),
        scratch_shapes=[pltpu.VMEM((s + 2 * _CONV_HALO, tc), F32)],
        compiler_params=_cparams(("parallel", "parallel"), 40),
        name="dn_conv",
    )(p, conv_w)


def _split3(x):
    x1 = x.astype(BF16)
    r1 = x - x1.astype(F32)
    x2 = r1.astype(BF16)
    x3 = (r1 - x2.astype(F32)).astype(BF16)
    return x1, x2, x3


def _gates_kernel(ba_ref, a_ref, dt_ref, kind_ref, dir_ref, o_ref, *, chunk, tl):
    x = ba_ref[0]
    beta = jax.nn.sigmoid(x)
    y = x + dt_ref[...]
    softplus = jnp.maximum(y, 0.0) + jnp.log(1.0 + jnp.exp(-jnp.abs(y)))
    g = -jnp.exp(a_ref[...]) * softplus
    i = lax.broadcasted_iota(jnp.int32, (tl, tl), 0)
    j = lax.broadcasted_iota(jnp.int32, (tl, tl), 1)
    sh = int(math.log2(chunk))
    same = lax.shift_right_logical(i, sh) == lax.shift_right_logical(j, sh)
    pre = jnp.where(same & (i <= j), 1.0, 0.0).astype(BF16)
    suf = jnp.where(same & (i >= j), 1.0, 0.0).astype(BF16)
    g1, g2, g3 = _split3(g)
    cpre = _dot(g1, pre) + _dot(g2, pre) + _dot(g3, pre)
    csuf = _dot(g1, suf) + _dot(g2, suf) + _dot(g3, suf)
    cum = jnp.where(dir_ref[...] > 0.5, csuf, cpre)
    o_ref[0] = jnp.where(kind_ref[...] > 0.5, cum, beta)


def _gates(ba_t, a_col, dt_col, kind_col, dir_col, chunk):
    b, r, s = ba_t.shape
    tl = min(512, s)
    col = pl.BlockSpec((r, 1), lambda i, j: (0, 0))
    return pl.pallas_call(
        functools.partial(_gates_kernel, chunk=chunk, tl=tl),
        out_shape=jax.ShapeDtypeStruct((b, r, s), F32),
        grid=(b, s // tl),
        in_specs=[pl.BlockSpec((1, r, tl), lambda i, j: (i, 0, j)), col, col, col, col],
        out_specs=pl.BlockSpec((1, r, tl), lambda i, j: (i, 0, j)),
        compiler_params=_cparams(("parallel", "parallel"), 32),
        name="dn_gates",
    )(ba_t, a_col, dt_col, kind_col, dir_col)


def _col_form(gr_ref, dst_ref, n_rows, chunk):
    pad = jnp.zeros((LANES - 8, LANES), F32)
    for s in range(n_rows // LANES):
        if chunk <= LANES:
            per = LANES // chunk
            parts = [gr_ref[0, 0, s * per + t] for t in range(per)]
            slab = parts[0] if per == 1 else jnp.concatenate(parts, axis=-1)
        else:
            per = chunk // LANES
            slab = gr_ref[0, 0, s // per, :, (s % per) * LANES:(s % per + 1) * LANES]
        dst_ref[s * LANES:(s + 1) * LANES, :] = jnp.concatenate([slab, pad], axis=0).T


def _delta_kernel(qc_ref, kc_ref, vc_ref, grc_ref, ql_ref, kl_ref, vl_ref, grl_ref,
                  z_ref, ng_ref, o_ref, colc_ref, coll_ref, s_ref, oacc_ref, *,
                  chunk, seq_c, seq_l):
    C = chunk
    n_c, n_l = seq_c // C, seq_l // C
    _col_form(grc_ref, colc_ref, seq_c, C)
    _col_form(grl_ref, coll_ref, seq_l, C)
    s_ref[...] = jnp.zeros_like(s_ref)
    oacc_ref[...] = jnp.zeros_like(oacc_ref)

    row = lax.broadcasted_iota(jnp.int32, (C, C), 0)
    col = lax.broadcasted_iota(jnp.int32, (C, C), 1)
    eye = jnp.where(row == col, 1.0, 0.0)

    def same_block(n):
        return lax.shift_right_logical(row, n) == lax.shift_right_logical(col, n)

    pair_mask = jnp.where(same_block(1), 1.0, 0.0)
    join_masks = [jnp.where(same_block(n + 1), 1.0, 0.0) - jnp.where(same_block(n), 1.0, 0.0)
                  for n in range(1, int(math.log2(C)))]

    def unit_tri_inverse(lm):
        x = eye - lm * pair_mask
        for jm in join_masks:
            y = _dot((lm * jm).astype(BF16), x.astype(BF16))
            x = x - _dot(x.astype(BF16), y.astype(BF16))
        return x

    def chain_step(d, e, ci, q_ref, k_ref, v_ref, gr_ref, col_ref, gram, qk, want_out):
        c = 2 * d + e
        r0 = pl.multiple_of(ci * C, C)
        kf = k_ref[0, pl.ds(r0, C), :].astype(F32)
        vf = v_ref[0, pl.ds(r0, C), e * DV:(e + 1) * DV].astype(F32)
        cols = col_ref[pl.ds(r0, C), :]
        beta = cols[:, c:c + 1]
        gcol = cols[:, 4 + c:5 + c]
        grow = gr_ref[0, 0, ci, 4 + c:5 + c, :]
        incl = (row >= col) if d == 0 else (row <= col)
        strict = (row > col) if d == 0 else (row < col)
        decay = jnp.exp(jnp.where(incl, gcol - grow, -jnp.inf))
        p = unit_tri_inverse(jnp.where(strict, gram * decay * beta, 0.0))
        eg = jnp.exp(gcol)
        rhs = jnp.concatenate([vf * beta, kf * (beta * eg)], axis=-1).astype(BF16)
        sol = _dot(p.astype(BF16), rhs)
        u, w = sol[:, :DV], sol[:, DV:]
        glast = gcol[C - 1:C, :] if d == 0 else gcol[0:1, :]
        ktail = (kf * jnp.exp(glast - gcol)).astype(BF16)
        state = s_ref[c]
        sb = state.astype(BF16)
        if want_out:
            qdec = q_ref[0, pl.ds(r0, C), :].astype(F32) * eg
            wq = _dot(jnp.concatenate([w, qdec], axis=0).astype(BF16), sb)
            vnew = u - wq[:C]
            out = wq[C:] + _dot((qk * decay).astype(BF16), vnew.astype(BF16))
            oacc_ref[pl.ds(r0, C), e * DV:(e + 1) * DV] += out
        else:
            vnew = u - _dot(w.astype(BF16), sb)
        s_ref[c] = state * jnp.exp(glast) + _dot_tn(ktail, vnew.astype(BF16))

    def make_body(n, q_ref, k_ref, v_ref, gr_ref, col_ref, want_out):
        def body(i, carry):
            for d in range(2):
                ci = i if d == 0 else n - 1 - i
                r0 = pl.multiple_of(ci * C, C)
                kb = k_ref[0, pl.ds(r0, C), :]
                gram = _dot_nt(kb, kb)
                qk = _dot_nt(q_ref[0, pl.ds(r0, C), :], kb) if want_out else None
                for e in range(2):
                    chain_step(d, e, ci, q_ref, k_ref, v_ref, gr_ref, col_ref, gram, qk,
                               want_out)
            return carry
        return body

    lax.fori_loop(0, n_c, make_body(n_c, qc_ref, kc_ref, vc_ref, grc_ref, colc_ref, False), 0)
    lax.fori_loop(0, n_l, make_body(n_l, ql_ref, kl_ref, vl_ref, grl_ref, coll_ref, True), 0)

    rows = min(512, seq_l)

    def epilogue(t, carry):
        base = pl.multiple_of(t * rows, 8)
        for e in range(2):
            o = oacc_ref[pl.ds(base, rows), e * DV:(e + 1) * DV]
            ms = jnp.mean(o * o, axis=-1, keepdims=True)
            y = o * lax.rsqrt(ms + EPS) * ng_ref[...]
            zz = z_ref[0, pl.ds(base, rows), e * DV:(e + 1) * DV]
            o_ref[0, pl.ds(base, rows), e * DV:(e + 1) * DV] = (y * _silu(zz)).astype(o_ref.dtype)
        return carry

    lax.fori_loop(0, seq_l // rows, epilogue, 0)


def _delta(qc, kc, vc, grc, ql, kl, vl, grl, p_lat, z_col0, norm_g, chunk):
    b, seq_c, _ = qc.shape
    seq_l = ql.shape[1]
    n_c, n_l = seq_c // chunk, seq_l // chunk
    zoff = z_col0 // (2 * DV)

    def tok(seq, width):
        return pl.BlockSpec((1, seq, width), lambda i, h: (i, 0, h))

    def gate_spec(n):
        return pl.BlockSpec((1, 1, n, 8, chunk), lambda i, h: (i, h, 0, 0, 0))

    return pl.pallas_call(
        functools.partial(_delta_kernel, chunk=chunk, seq_c=seq_c, seq_l=seq_l),
        out_shape=jax.ShapeDtypeStruct((b, seq_l, HV * DV), BF16),
        grid=(b, HK),
        in_specs=[tok(seq_c, DK), tok(seq_c, DK), tok(seq_c, 2 * DV), gate_spec(n_c),
                  tok(seq_l, DK), tok(seq_l, DK), tok(seq_l, 2 * DV), gate_spec(n_l),
                  pl.BlockSpec((1, seq_l, 2 * DV), lambda i, h: (i, 0, h + zoff)),
                  pl.BlockSpec((1, DV), lambda i, h: (0, 0))],
        out_specs=tok(seq_l, 2 * DV),
        scratch_shapes=[pltpu.VMEM((seq_c, LANES), F32),
                        pltpu.VMEM((seq_l, LANES), F32),
                        pltpu.VMEM((4, DK, DV), F32),
                        pltpu.VMEM((seq_l, 2 * DV), F32)],
        compiler_params=_cparams(("parallel", "parallel"), 48),
        name="dn_delta",
    )(qc, kc, vc, grc, ql, kl, vl, grl, p_lat, norm_g.reshape(1, DV))


def _chan_dft_kernel(tab_ref, w_ref, o_ref):
    o_ref[...] = _dot(tab_ref[0], w_ref[...]).astype(o_ref.dtype)


def _chan_dft_weights(tabs, w):
    d = w.shape[0]
    g = tabs.shape[1]
    tn = 512
    nj = d // tn
    return pl.pallas_call(
        _chan_dft_kernel,
        out_shape=jax.ShapeDtypeStruct((d, 2 * d), BF16),
        grid=(d // g, 2, nj),
        in_specs=[pl.BlockSpec((1, g, g), lambda i, t, j: (t, 0, 0)),
                  pl.BlockSpec((g, tn), lambda i, t, j: (i, j))],
        out_specs=pl.BlockSpec((g, tn), lambda i, t, j: (i, t * nj + j)),
        compiler_params=_cparams(("parallel", "parallel", "parallel"), 32),
        name="fourier_chan",
    )(tabs, w)


def _seq_dft_kernel(cos_ref, sin_ref, a_ref, b_ref, mod_ref, r_ref, o_ref, *, scale):
    acc = _dot(cos_ref[...], a_ref[0]) - _dot(sin_ref[...], b_ref[0])
    o_ref[0] = r_ref[0] + (mod_ref[0, 2:3, :] * scale) * acc


def _seq_dft(cos_t, sin_t, ab, modv, resid, scale):
    b, seq, d2 = ab.shape
    d = d2 // 2
    tm, tn = 512, 512
    nj = d // tn
    return pl.pallas_call(
        functools.partial(_seq_dft_kernel, scale=scale),
        out_shape=jax.ShapeDtypeStruct((b, seq, d), F32),
        grid=(b, seq // tm, nj),
        in_specs=[pl.BlockSpec((tm, seq), lambda n, i, j: (i, 0)),
                  pl.BlockSpec((tm, seq), lambda n, i, j: (i, 0)),
                  pl.BlockSpec((1, seq, tn), lambda n, i, j: (n, 0, j)),
                  pl.BlockSpec((1, seq, tn), lambda n, i, j: (n, 0, j + nj)),
                  pl.BlockSpec((1, 8, tn), lambda n, i, j: (n, 0, j)),
                  pl.BlockSpec((1, tm, tn), lambda n, i, j: (n, i, j))],
        out_specs=pl.BlockSpec((1, tm, tn), lambda n, i, j: (n, i, j)),
        compiler_params=_cparams(("parallel", "parallel", "arbitrary"), 48),
        name="fourier_seq",
    )(cos_t, sin_t, ab, ab, modv, resid)


def _dft_tables(n):
    k = jnp.arange(n, dtype=jnp.int32)
    ang = ((k[:, None] * k[None, :]) % n).astype(F32) * (2.0 * math.pi / n)
    return jnp.cos(ang).astype(BF16), jnp.sin(ang).astype(BF16)


def _gate_rows(p_ba, chunk):
    b, s, _ = p_ba.shape
    t = p_ba.reshape(b, s, 2, 2, HK, 2)
    return jnp.transpose(t, (0, 4, 2, 3, 5, 1)).reshape(b, 4 * HV, s)


def _gate_params(a_log, dt_bias):
    def rows(v, kind_fill):
        t = jnp.stack([jnp.full_like(v, kind_fill), v]).reshape(2, 2, HK, 2)
        return jnp.transpose(t, (2, 0, 1, 3)).reshape(4 * HV, 1)
    kind = rows(jnp.ones((2, HV), F32), 0.0)
    dirn = rows(jnp.stack([jnp.zeros((HV,), F32), jnp.ones((HV,), F32)]), 0.0)
    return rows(a_log.astype(F32), 0.0), rows(dt_bias.astype(F32), 0.0), kind, dirn


def _chunk_major(rows_t, chunk):
    b, _, s = rows_t.shape
    t = rows_t.reshape(b, HK, 8, s // chunk, chunk)
    return jnp.transpose(t, (0, 1, 3, 2, 4))


def kernel(x, c, ctx, c_ctx, norm_g, mod_w, mod_b, ffn_w_gate, ffn_w_up, ffn_w_down, dn_w_in,
           dn_conv_w, dn_a_log, dn_dt_bias, dn_norm_g, dn_w_out, fn_w_out, final_norm_g):
    bsz, seq, d = x.shape
    seq_c = ctx.shape[1]
    t_lat, t_ctx = bsz * seq, bsz * seq_c
    key_dim, val_dim = HK * DK, HV * DV
    qkv_dim = 2 * key_dim + val_dim
    chunk = DELTA_CHUNK

    cond = jnp.zeros((8, d), F32).at[:bsz].set(c).at[bsz].set(c_ctx)
    mod = _modulation(cond, mod_w, mod_b)

    def modv(layer, sub, ctx_rows=False):
        m3 = mod[layer, :, 3 * sub * d:3 * (sub + 1) * d].reshape(8, 3, d)
        m3 = m3[bsz:bsz + 1] if ctx_rows else m3[:bsz]
        return jnp.pad(m3, ((0, 0), (0, 5), (0, 0)))

    def ffn_weights(layer, which):
        return (ffn_w_gate[layer, which].astype(BF16), ffn_w_up[layer, which].astype(BF16),
                ffn_w_down[layer, which].astype(BF16))

    h = x.reshape(t_lat, d)
    hc = ctx.reshape(t_ctx, d)

    wa = ffn_weights(0, 0)
    h = _ffn(h, modv(0, 0), norm_g[0, 0], *wa)
    hc = _ffn(hc, modv(0, 0, True), norm_g[0, 0], *wa)

    n_in = dn_w_in.shape[2]
    n_pad = -(-n_in // 512) * 512
    w_in = jnp.pad(dn_w_in[0].astype(BF16), ((0, 0), (0, n_pad - n_in)))
    p = _matmul(h, w_in, out_dtype=F32, tm=1024, tn=512, modv=modv(0, 1), norm_g=norm_g[0, 1],
                name="dn_in_proj").reshape(bsz, seq, n_pad)
    pc = _matmul(hc, w_in, out_dtype=F32, tm=512, tn=512, modv=modv(0, 1, True),
                 norm_g=norm_g[0, 1], name="dn_in_proj_ctx").reshape(bsz, seq_c, n_pad)

    conv_w = dn_conv_w[0]
    a_col, dt_col, kind_col, dir_col = _gate_params(dn_a_log[0], dn_dt_bias[0])

    def dn_inputs(pp):
        q = _conv_act(pp, conv_w, 0, key_dim, l2norm=True, out_scale=DK ** -0.5)
        k = _conv_act(pp, conv_w, key_dim, key_dim, l2norm=True)
        v = _conv_act(pp, conv_w, 2 * key_dim, val_dim, l2norm=False)
        ba = pp[:, :, qkv_dim + val_dim:qkv_dim + val_dim + 4 * HV]
        gr = _gates(_gate_rows(ba, chunk), a_col, dt_col, kind_col, dir_col, chunk)
        return q, k, v, _chunk_major(gr, chunk)

    ql, kl, vl, grl = dn_inputs(p)
    qc, kc, vc, grc = dn_inputs(pc)
    y = _delta(qc, kc, vc, grc, ql, kl, vl, grl, p, qkv_dim, dn_norm_g[0], chunk)
    h = _matmul(y.reshape(t_lat, val_dim), dn_w_out[0].astype(BF16), out_dtype=F32, tm=1024,
                tn=512, modv=modv(0, 1), resid=h, name="dn_out_proj")
    h = _ffn(h, modv(0, 2), norm_g[0, 2], *ffn_weights(0, 1))

    h = _ffn(h, modv(1, 0), norm_g[1, 0], *ffn_weights(1, 0))
    gdim = d // N_FGROUPS
    cos_c, sin_c = _dft_tables(gdim)
    w_cs = _chan_dft_weights(jnp.stack([cos_c, sin_c]), fn_w_out[0].astype(BF16))
    ab = _matmul(h, w_cs, out_dtype=BF16, tm=1024, tn=512, modv=modv(1, 1), norm_g=norm_g[1, 1],
                 name="fourier_in").reshape(bsz, seq, 2 * d)
    cos_l, sin_l = _dft_tables(seq)
    h = _seq_dft(cos_l, sin_l, ab, modv(1, 1), h.reshape(bsz, seq, d),
                 1.0 / math.sqrt(seq * gdim)).reshape(t_lat, d)
    h = _ffn(h, modv(1, 2), norm_g[1, 2], *ffn_weights(1, 1), final_g=final_norm_g)
    return h.reshape(bsz, seq, d)
```

```python
import functools
import math

import jax
import jax.numpy as jnp
from jax import lax
from jax.experimental import pallas as pl
from jax.experimental.pallas import tpu as pltpu

F32 = jnp.float32
BF16 = jnp.bfloat16

EPS = 1e-6
FFN_RES = 0.5
N_SUB = 3
HK = 16
HV = 32
DK = 128
DV = 128
CONV_W = 5
N_FGROUPS = 4
DELTA_CHUNK = 128
PHASE_A_GROUP = 4

LANES = 128
MIB = 1024 * 1024


def _cparams(semantics, vmem_mib):
    return pltpu.CompilerParams(dimension_semantics=semantics,
                                vmem_limit_bytes=int(vmem_mib * MIB))


def _dot(a, b):
    return jnp.dot(a, b, preferred_element_type=F32)


def _dot_nt(a, b):
    return lax.dot_general(a, b, (((1,), (1,)), ((), ())), preferred_element_type=F32)


def _dot_tn(a, b):
    return lax.dot_general(a, b, (((0,), (0,)), ((), ())), preferred_element_type=F32)


def _silu(x):
    return x * jax.nn.sigmoid(x)


def _ada_norm(x, g, shift, scale):
    ms = jnp.mean(x * x, axis=-1, keepdims=True)
    return (x * lax.rsqrt(ms + EPS) * g) * (1.0 + scale) + shift


def _mod_kernel(cond_ref, w_ref, b_ref, o_ref):
    s = _silu(cond_ref[...]).astype(BF16)
    o_ref[0] = _dot(s, w_ref[0].astype(BF16)) + b_ref[0]


def _modulation(cond, mod_w, mod_b):
    depth, d, n = mod_w.shape
    tn = 1024
    return pl.pallas_call(
        _mod_kernel,
        out_shape=jax.ShapeDtypeStruct((depth, 8, n), F32),
        grid=(depth, n // tn),
        in_specs=[pl.BlockSpec((8, d), lambda l, j: (0, 0)),
                  pl.BlockSpec((1, d, tn), lambda l, j: (l, 0, j)),
                  pl.BlockSpec((1, 1, tn), lambda l, j: (l, 0, j))],
        out_specs=pl.BlockSpec((1, 8, tn), lambda l, j: (l, 0, j)),
        compiler_params=_cparams(("parallel", "parallel"), 40),
        name="modulation",
    )(cond, mod_w, mod_b.reshape(depth, 1, n))


def _ffn_kernel(x_ref, mod_ref, g_ref, wg_ref, wu_ref, wd_ref, fg_ref, o_ref, h_ref, *,
                final_norm):
    j = pl.program_id(1)
    nj = pl.num_programs(1)

    @pl.when(j == 0)
    def _():
        h_ref[...] = _ada_norm(x_ref[...], g_ref[...], mod_ref[0, 0:1, :],
                               mod_ref[0, 1:2, :]).astype(BF16)

    h = h_ref[...]
    a = (_silu(_dot(h, wg_ref[...])) * _dot(h, wu_ref[...])).astype(BF16)
    part = _dot(a, wd_ref[...])

    @pl.when(j == 0)
    def _():
        o_ref[...] = part

    @pl.when(j > 0)
    def _():
        o_ref[...] += part

    @pl.when(j == nj - 1)
    def _():
        r = x_ref[...] + (FFN_RES * mod_ref[0, 2:3, :]) * o_ref[...]
        if final_norm:
            ms = jnp.mean(r * r, axis=-1, keepdims=True)
            r = r * lax.rsqrt(ms + EPS) * fg_ref[...]
        o_ref[...] = r


def _ffn(x, modv, g, wg, wu, wd, final_g=None):
    t, d = x.shape
    f = wg.shape[1]
    tm = min(512, t)
    tf = 512
    tiles_per_mod = t // tm // modv.shape[0]
    fg = (final_g if final_g is not None else g).reshape(1, d)
    return pl.pallas_call(
        functools.partial(_ffn_kernel, final_norm=final_g is not None),
        out_shape=jax.ShapeDtypeStruct((t, d), F32),
        grid=(t // tm, f // tf),
        in_specs=[pl.BlockSpec((tm, d), lambda i, j: (i, 0)),
                  pl.BlockSpec((1, 8, d), lambda i, j: (i // tiles_per_mod, 0, 0)),
                  pl.BlockSpec((1, d), lambda i, j: (0, 0)),
                  pl.BlockSpec((d, tf), lambda i, j: (0, j)),
                  pl.BlockSpec((d, tf), lambda i, j: (0, j)),
                  pl.BlockSpec((tf, d), lambda i, j: (j, 0)),
                  pl.BlockSpec((1, d), lambda i, j: (0, 0))],
        out_specs=pl.BlockSpec((tm, d), lambda i, j: (i, 0)),
        scratch_shapes=[pltpu.VMEM((tm, d), BF16)],
        compiler_params=_cparams(("parallel", "arbitrary"), 48),
        name="ffn_half",
    )(x, modv, g.reshape(1, d), wg, wu, wd, fg)


def _mm_kernel(*refs, norm, resid):
    it = iter(refs)
    a_ref = next(it)
    b_ref = next(it)
    if norm:
        mod_ref, g_ref = next(it), next(it)
    elif resid:
        mod_ref = next(it)
    if resid:
        r_ref = next(it)
    o_ref = next(it)

    if norm:
        h_ref = next(it)

        @pl.when(pl.program_id(1) == 0)
        def _():
            h_ref[...] = _ada_norm(a_ref[...], g_ref[...], mod_ref[0, 0:1, :],
                                   mod_ref[0, 1:2, :]).astype(BF16)
        a = h_ref[...]
    else:
        a = a_ref[...]
    acc = _dot(a, b_ref[...])
    if resid:
        acc = r_ref[...] + mod_ref[0, 2:3, :] * acc
    o_ref[...] = acc.astype(o_ref.dtype)


def _matmul(a, b, *, out_dtype, tm, tn, modv=None, norm_g=None, resid=None, vmem_mib=48,
            name="matmul"):
    m, k = a.shape
    n = b.shape[1]
    tm = min(tm, m)
    norm = norm_g is not None
    has_res = resid is not None
    in_specs = [pl.BlockSpec((tm, k), lambda i, j: (i, 0)),
                pl.BlockSpec((k, tn), lambda i, j: (0, j))]
    args = [a, b]
    if norm or has_res:
        tiles_per_mod = m // tm // modv.shape[0]
        if norm:
            in_specs.append(pl.BlockSpec((1, 8, k), lambda i, j: (i // tiles_per_mod, 0, 0)))
            in_specs.append(pl.BlockSpec((1, k), lambda i, j: (0, 0)))
            args += [modv, norm_g.reshape(1, k)]
        else:
            in_specs.append(pl.BlockSpec((1, 8, tn), lambda i, j: (i // tiles_per_mod, 0, j)))
            args.append(modv)
    if has_res:
        in_specs.append(pl.BlockSpec((tm, tn), lambda i, j: (i, j)))
        args.append(resid)
    return pl.pallas_call(
        functools.partial(_mm_kernel, norm=norm, resid=has_res),
        out_shape=jax.ShapeDtypeStruct((m, n), out_dtype),
        grid=(m // tm, n // tn),
        in_specs=in_specs,
        out_specs=pl.BlockSpec((tm, tn), lambda i, j: (i, j)),
        scratch_shapes=[pltpu.VMEM((tm, k), BF16)] if norm else [],
        compiler_params=_cparams(("parallel", "arbitrary"), vmem_mib),
        name=name,
    )(*args)


_CONV_ROWS = 256
_CONV_HALO = 8


def _conv_kernel(x_ref, w_ref, o_ref, xp_ref, *, seq, tc, l2norm, out_scale):
    zeros = jnp.zeros((_CONV_HALO, tc), F32)
    xp_ref[0:_CONV_HALO, :] = zeros
    xp_ref[_CONV_HALO + seq:, :] = zeros
    xp_ref[_CONV_HALO:_CONV_HALO + seq, :] = x_ref[0]
    w = w_ref[...]
    rows = min(_CONV_ROWS, seq)
    win_rows = rows + 2 * _CONV_HALO

    def step(t, carry):
        base = pl.multiple_of(t * rows, 8)
        win = xp_ref[pl.ds(base, win_rows), :]
        acc = None
        for j in range(CONV_W):
            shift = (CONV_W // 2 - j) % win_rows
            rolled = win if shift == 0 else pltpu.roll(win, shift, 0)
            term = rolled[_CONV_HALO:_CONV_HALO + rows, :] * w[j:j + 1, :]
            acc = term if acc is None else acc + term
        y = _silu(acc)
        if l2norm:
            parts = []
            for hh in range(tc // LANES):
                yh = y[:, hh * LANES:(hh + 1) * LANES]
                ss = jnp.sum(yh * yh, axis=-1, keepdims=True)
                parts.append(yh * (lax.rsqrt(ss + EPS) * out_scale))
            y = parts[0] if len(parts) == 1 else jnp.concatenate(parts, axis=-1)
        o_ref[0, pl.ds(base, rows), :] = y.astype(o_ref.dtype)
        return carry

    lax.fori_loop(0, seq // rows, step, 0)


def _conv_act(p, conv_w, col0, ncols, *, l2norm, out_scale=1.0):
    b, s, _ = p.shape
    tc = 256
    off = col0 // tc
    return pl.pallas_call(
        functools.partial(_conv_kernel, seq=s, tc=tc, l2norm=l2norm, out_scale=out_scale),
        out_shape=jax.ShapeDtypeStruct((b, s, ncols), BF16),
        grid=(b, ncols // tc),
        in_specs=[pl.BlockSpec((1, s, tc), lambda i, j: (i, 0, j + off)),
                  pl.BlockSpec((CONV_W, tc), lambda i, j: (0, j + off))],
        out_specs=pl.BlockSpec((1, s, tc), lambda i, j: (i, 0, j)),
        scratch_shapes=[pltpu.VMEM((s + 2 * _CONV_HALO, tc), F32)],
        compiler_params=_cparams(("parallel", "parallel"), 40),
        name="dn_conv",
    )(p, conv_w)


def _split3(x):
    x1 = x.astype(BF16)
    r1 = x - x1.astype(F32)
    x2 = r1.astype(BF16)
    x3 = (r1 - x2.astype(F32)).astype(BF16)
    return x1, x2, x3


def _gates_kernel(ba_ref, a_ref, dt_ref, kind_ref, dir_ref, o_ref, *, chunk, tl):
    x = ba_ref[0]
    beta = jax.nn.sigmoid(x)
    y = x + dt_ref[...]
    softplus = jnp.maximum(y, 0.0) + jnp.log(1.0 + jnp.exp(-jnp.abs(y)))
    g = -jnp.exp(a_ref[...]) * softplus
    i = lax.broadcasted_iota(jnp.int32, (tl, tl), 0)
    j = lax.broadcasted_iota(jnp.int32, (tl, tl), 1)
    sh = int(math.log2(chunk))
    same = lax.shift_right_logical(i, sh) == lax.shift_right_logical(j, sh)
    pre = jnp.where(same & (i <= j), 1.0, 0.0).astype(BF16)
    suf = jnp.where(same & (i >= j), 1.0, 0.0).astype(BF16)
    g1, g2, g3 = _split3(g)
    cpre = _dot(g1, pre) + _dot(g2, pre) + _dot(g3, pre)
    csuf = _dot(g1, suf) + _dot(g2, suf) + _dot(g3, suf)
    cum = jnp.where(dir_ref[...] > 0.5, csuf, cpre)
    o_ref[0] = jnp.where(kind_ref[...] > 0.5, cum, beta)


def _gates(ba_t, a_col, dt_col, kind_col, dir_col, chunk):
    b, r, s = ba_t.shape
    tl = min(512, s)
    col = pl.BlockSpec((r, 1), lambda i, j: (0, 0))
    return pl.pallas_call(
        functools.partial(_gates_kernel, chunk=chunk, tl=tl),
        out_shape=jax.ShapeDtypeStruct((b, r, s), F32),
        grid=(b, s // tl),
        in_specs=[pl.BlockSpec((1, r, tl), lambda i, j: (i, 0, j)), col, col, col, col],
        out_specs=pl.BlockSpec((1, r, tl), lambda i, j: (i, 0, j)),
        compiler_params=_cparams(("parallel", "parallel"), 32),
        name="dn_gates",
    )(ba_t, a_col, dt_col, kind_col, dir_col)


def _col_form(gr_ref, dst_ref, n_rows, chunk):
    pad = jnp.zeros((LANES - 8, LANES), F32)
    for s in range(n_rows // LANES):
        if chunk <= LANES:
            per = LANES // chunk
            parts = [gr_ref[0, 0, s * per + t] for t in range(per)]
            slab = parts[0] if per == 1 else jnp.concatenate(parts, axis=-1)
        else:
            per = chunk // LANES
            slab = gr_ref[0, 0, s // per, :, (s % per) * LANES:(s % per + 1) * LANES]
        dst_ref[s * LANES:(s + 1) * LANES, :] = jnp.concatenate([slab, pad], axis=0).T


def _delta_kernel(qc_ref, kc_ref, vc_ref, grc_ref, ql_ref, kl_ref, vl_ref, grl_ref,
                  z_ref, ng_ref, o_ref, colc_ref, coll_ref, aq_ref, b_ref, gl_ref, s_ref,
                  oacc_ref, *, chunk, seq_c, seq_l):
    C = chunk
    n_c, n_l = seq_c // C, seq_l // C
    _col_form(grc_ref, colc_ref, seq_c, C)
    _col_form(grl_ref, coll_ref, seq_l, C)
    s_ref[...] = jnp.zeros_like(s_ref)
    oacc_ref[...] = jnp.zeros_like(oacc_ref)

    row = lax.broadcasted_iota(jnp.int32, (C, C), 0)
    col = lax.broadcasted_iota(jnp.int32, (C, C), 1)
    eye = jnp.where(row == col, 1.0, 0.0)

    def same_block(n):
        return lax.shift_right_logical(row, n) == lax.shift_right_logical(col, n)

    pair_mask = jnp.where(same_block(1), 1.0, 0.0)
    join_masks = [jnp.where(same_block(n + 1), 1.0, 0.0) - jnp.where(same_block(n), 1.0, 0.0)
                  for n in range(1, int(math.log2(C)))]

    chains = range(4)

    def phase_a(q_ref, k_ref, v_ref, gr_ref, col_ref, slot0, want_out, group):
        items = [(g, c) for g in range(group) for c in chains]

        def body(it, carry):
            cis = [it * group + g for g in range(group)]
            r0s = [pl.multiple_of(ci * C, C) for ci in cis]
            kb = [k_ref[0, pl.ds(r0, C), :] for r0 in r0s]
            kf = [t.astype(F32) for t in kb]
            gram = [_dot_nt(t, t) for t in kb]
            if want_out:
                qb = [q_ref[0, pl.ds(r0, C), :] for r0 in r0s]
                qk = [_dot_nt(qb[g], kb[g]) for g in range(group)]
            cols = [col_ref[pl.ds(r0, C), :] for r0 in r0s]
            beta, gcol, decay, lm, x = [], [], [], [], []
            for n, (g, c) in enumerate(items):
                d = c // 2
                beta.append(cols[g][:, c:c + 1])
                gcol.append(cols[g][:, 4 + c:5 + c])
                grow = gr_ref[0, 0, cis[g], 4 + c:5 + c, :]
                incl = (row >= col) if d == 0 else (row <= col)
                strict = (row > col) if d == 0 else (row < col)
                decay.append(jnp.exp(jnp.where(incl, gcol[n] - grow, -jnp.inf)))
                lm.append(jnp.where(strict, gram[g] * decay[n] * beta[n], 0.0))
                x.append(eye - lm[n] * pair_mask)
            idx = range(len(items))
            for jm in join_masks:
                y = [_dot((lm[n] * jm).astype(BF16), x[n].astype(BF16)) for n in idx]
                x = [x[n] - _dot(x[n].astype(BF16), y[n].astype(BF16)) for n in idx]
            eg = [jnp.exp(gcol[n]) for n in idx]
            sol = []
            for n, (g, c) in enumerate(items):
                e = c % 2
                vf = v_ref[0, pl.ds(r0s[g], C), e * DV:(e + 1) * DV].astype(F32)
                rhs = jnp.concatenate([vf * beta[n], kf[g] * (beta[n] * eg[n])], axis=-1)
                sol.append(_dot(x[n].astype(BF16), rhs.astype(BF16)).astype(BF16))
            kw = []
            for n, (g, c) in enumerate(items):
                glast = gcol[n][C - 1:C, :] if c // 2 == 0 else gcol[n][0:1, :]
                ktail = (kf[g] * jnp.exp(glast - gcol[n])).astype(BF16)
                kw.append(_dot_tn(ktail, sol[n]))
                gl_ref[c, slot0 + cis[g]] = jnp.broadcast_to(jnp.exp(glast), (8, LANES))
            if want_out:
                aw = [_dot((qk[g] * decay[n]).astype(BF16), sol[n])
                      for n, (g, c) in enumerate(items)]
            for n, (g, c) in enumerate(items):
                slot = slot0 + cis[g]
                b_ref[c, slot] = kw[n][:, :DV]
                aq_ref[c, slot, 0:DK, :] = (-kw[n][:, DV:]).astype(BF16)
                if want_out:
                    aq_ref[c, slot, DK:DK + C, :] = (qb[g].astype(F32) * eg[n]
                                                     - aw[n][:, DV:]).astype(BF16)
            if want_out:
                for g in range(group):
                    for e in range(2):
                        oacc_ref[pl.ds(r0s[g], C), e * DV:(e + 1) * DV] += (
                            aw[4 * g + e][:, :DV] + aw[4 * g + 2 + e][:, :DV])
            return carry
        return body

    def phase_b(n, slot0, want_out):
        def body(i, carry):
            states = [s_ref[c] for c in chains]
            cis = [i if c // 2 == 0 else n - 1 - i for c in chains]
            res = []
            for c in chains:
                slot = slot0 + cis[c]
                lhs = aq_ref[c, slot] if want_out else aq_ref[c, slot, 0:DK, :]
                res.append(_dot(lhs, states[c].astype(BF16)))
            for c in chains:
                slot = slot0 + cis[c]
                s_ref[c] = states[c] * gl_ref[c, slot, 0:1, :] + res[c][:DK] + b_ref[c, slot]
                if want_out:
                    r0 = pl.multiple_of(cis[c] * C, C)
                    e = c % 2
                    oacc_ref[pl.ds(r0, C), e * DV:(e + 1) * DV] += res[c][DK:]
            return carry
        return body

    grp_c = math.gcd(n_c, PHASE_A_GROUP)
    grp_l = math.gcd(n_l, PHASE_A_GROUP)
    lax.fori_loop(0, n_c // grp_c,
                  phase_a(qc_ref, kc_ref, vc_ref, grc_ref, colc_ref, 0, False, grp_c), 0)
    lax.fori_loop(0, n_l // grp_l,
                  phase_a(ql_ref, kl_ref, vl_ref, grl_ref, coll_ref, n_c, True, grp_l), 0)
    lax.fori_loop(0, n_c, phase_b(n_c, 0, False), 0)
    lax.fori_loop(0, n_l, phase_b(n_l, n_c, True), 0)

    rows = min(512, seq_l)

    def epilogue(t, carry):
        base = pl.multiple_of(t * rows, 8)
        for e in range(2):
            o = oacc_ref[pl.ds(base, rows), e * DV:(e + 1) * DV]
            ms = jnp.mean(o * o, axis=-1, keepdims=True)
            y = o * lax.rsqrt(ms + EPS) * ng_ref[...]
            zz = z_ref[0, pl.ds(base, rows), e * DV:(e + 1) * DV]
            o_ref[0, pl.ds(base, rows), e * DV:(e + 1) * DV] = (y * _silu(zz)).astype(o_ref.dtype)
        return carry

    lax.fori_loop(0, seq_l // rows, epilogue, 0)


def _delta(qc, kc, vc, grc, ql, kl, vl, grl, p_lat, z_col0, norm_g, chunk):
    b, seq_c, _ = qc.shape
    seq_l = ql.shape[1]
    n_c, n_l = seq_c // chunk, seq_l // chunk
    zoff = z_col0 // (2 * DV)

    def tok(seq, width):
        return pl.BlockSpec((1, seq, width), lambda i, h: (i, 0, h))

    def gate_spec(n):
        return pl.BlockSpec((1, 1, n, 8, chunk), lambda i, h: (i, h, 0, 0, 0))

    return pl.pallas_call(
        functools.partial(_delta_kernel, chunk=chunk, seq_c=seq_c, seq_l=seq_l),
        out_shape=jax.ShapeDtypeStruct((b, seq_l, HV * DV), BF16),
        grid=(b, HK),
        in_specs=[tok(seq_c, DK), tok(seq_c, DK), tok(seq_c, 2 * DV), gate_spec(n_c),
                  tok(seq_l, DK), tok(seq_l, DK), tok(seq_l, 2 * DV), gate_spec(n_l),
                  pl.BlockSpec((1, seq_l, 2 * DV), lambda i, h: (i, 0, h + zoff)),
                  pl.BlockSpec((1, DV), lambda i, h: (0, 0))],
        out_specs=tok(seq_l, 2 * DV),
        scratch_shapes=[pltpu.VMEM((seq_c, LANES), F32),
                        pltpu.VMEM((seq_l, LANES), F32),
                        pltpu.VMEM((4, n_c + n_l, DK + chunk, DV), BF16),
                        pltpu.VMEM((4, n_c + n_l, DK, DV), F32),
                        pltpu.VMEM((4, n_c + n_l, 8, LANES), F32),
                        pltpu.VMEM((4, DK, DV), F32),
                        pltpu.VMEM((seq_l, 2 * DV), F32)],
        compiler_params=_cparams(("parallel", "parallel"), 56),
        name="dn_delta",
    )(qc, kc, vc, grc, ql, kl, vl, grl, p_lat, norm_g.reshape(1, DV))


def _chan_dft_kernel(tab_ref, w_ref, o_ref):
    o_ref[...] = _dot(tab_ref[0], w_ref[...]).astype(o_ref.dtype)


def _chan_dft_weights(tabs, w):
    d = w.shape[0]
    g = tabs.shape[1]
    tn = 512
    nj = d // tn
    return pl.pallas_call(
        _chan_dft_kernel,
        out_shape=jax.ShapeDtypeStruct((d, 2 * d), BF16),
        grid=(d // g, 2, nj),
        in_specs=[pl.BlockSpec((1, g, g), lambda i, t, j: (t, 0, 0)),
                  pl.BlockSpec((g, tn), lambda i, t, j: (i, j))],
        out_specs=pl.BlockSpec((g, tn), lambda i, t, j: (i, t * nj + j)),
        compiler_params=_cparams(("parallel", "parallel", "parallel"), 32),
        name="fourier_chan",
    )(tabs, w)


def _seq_dft_kernel(cos_ref, sin_ref, a_ref, b_ref, mod_ref, r_ref, o_ref, *, scale):
    acc = _dot(cos_ref[...], a_ref[0]) - _dot(sin_ref[...], b_ref[0])
    o_ref[0] = r_ref[0] + (mod_ref[0, 2:3, :] * scale) * acc


def _seq_dft(cos_t, sin_t, ab, modv, resid, scale):
    b, seq, d2 = ab.shape
    d = d2 // 2
    tm, tn = 512, 512
    nj = d // tn
    return pl.pallas_call(
        functools.partial(_seq_dft_kernel, scale=scale),
        out_shape=jax.ShapeDtypeStruct((b, seq, d), F32),
        grid=(b, seq // tm, nj),
        in_specs=[pl.BlockSpec((tm, seq), lambda n, i, j: (i, 0)),
                  pl.BlockSpec((tm, seq), lambda n, i, j: (i, 0)),
                  pl.BlockSpec((1, seq, tn), lambda n, i, j: (n, 0, j)),
                  pl.BlockSpec((1, seq, tn), lambda n, i, j: (n, 0, j + nj)),
                  pl.BlockSpec((1, 8, tn), lambda n, i, j: (n, 0, j)),
                  pl.BlockSpec((1, tm, tn), lambda n, i, j: (n, i, j))],
        out_specs=pl.BlockSpec((1, tm, tn), lambda n, i, j: (n, i, j)),
        compiler_params=_cparams(("parallel", "parallel", "arbitrary"), 48),
        name="fourier_seq",
    )(cos_t, sin_t, ab, ab, modv, resid)


def _dft_tables(n):
    k = jnp.arange(n, dtype=jnp.int32)
    ang = ((k[:, None] * k[None, :]) % n).astype(F32) * (2.0 * math.pi / n)
    return jnp.cos(ang).astype(BF16), jnp.sin(ang).astype(BF16)


def _gate_rows(p_ba, chunk):
    b, s, _ = p_ba.shape
    t = p_ba.reshape(b, s, 2, 2, HK, 2)
    return jnp.transpose(t, (0, 4, 2, 3, 5, 1)).reshape(b, 4 * HV, s)


def _gate_params(a_log, dt_bias):
    def rows(v, kind_fill):
        t = jnp.stack([jnp.full_like(v, kind_fill), v]).reshape(2, 2, HK, 2)
        return jnp.transpose(t, (2, 0, 1, 3)).reshape(4 * HV, 1)
    kind = rows(jnp.ones((2, HV), F32), 0.0)
    dirn = rows(jnp.stack([jnp.zeros((HV,), F32), jnp.ones((HV,), F32)]), 0.0)
    return rows(a_log.astype(F32), 0.0), rows(dt_bias.astype(F32), 0.0), kind, dirn


def _chunk_major(rows_t, chunk):
    b, _, s = rows_t.shape
    t = rows_t.reshape(b, HK, 8, s // chunk, chunk)
    return jnp.transpose(t, (0, 1, 3, 2, 4))


def kernel(x, c, ctx, c_ctx, norm_g, mod_w, mod_b, ffn_w_gate, ffn_w_up, ffn_w_down, dn_w_in,
           dn_conv_w, dn_a_log, dn_dt_bias, dn_norm_g, dn_w_out, fn_w_out, final_norm_g):
    bsz, seq, d = x.shape
    seq_c = ctx.shape[1]
    t_lat, t_ctx = bsz * seq, bsz * seq_c
    key_dim, val_dim = HK * DK, HV * DV
    qkv_dim = 2 * key_dim + val_dim
    chunk = DELTA_CHUNK

    cond = jnp.zeros((8, d), F32).at[:bsz].set(c).at[bsz].set(c_ctx)
    mod = _modulation(cond, mod_w, mod_b)

    def modv(layer, sub, ctx_rows=False):
        m3 = mod[layer, :, 3 * sub * d:3 * (sub + 1) * d].reshape(8, 3, d)
        m3 = m3[bsz:bsz + 1] if ctx_rows else m3[:bsz]
        return jnp.pad(m3, ((0, 0), (0, 5), (0, 0)))

    def ffn_weights(layer, which):
        return (ffn_w_gate[layer, which].astype(BF16), ffn_w_up[layer, which].astype(BF16),
                ffn_w_down[layer, which].astype(BF16))

    h = x.reshape(t_lat, d)
    hc = ctx.reshape(t_ctx, d)

    wa = ffn_weights(0, 0)
    h = _ffn(h, modv(0, 0), norm_g[0, 0], *wa)
    hc = _ffn(hc, modv(0, 0, True), norm_g[0, 0], *wa)

    n_in = dn_w_in.shape[2]
    n_pad = -(-n_in // 512) * 512
    w_in = jnp.pad(dn_w_in[0].astype(BF16), ((0, 0), (0, n_pad - n_in)))
    p = _matmul(h, w_in, out_dtype=F32, tm=1024, tn=512, modv=modv(0, 1), norm_g=norm_g[0, 1],
                name="dn_in_proj").reshape(bsz, seq, n_pad)
    pc = _matmul(hc, w_in, out_dtype=F32, tm=512, tn=512, modv=modv(0, 1, True),
                 norm_g=norm_g[0, 1], name="dn_in_proj_ctx").reshape(bsz, seq_c, n_pad)

    conv_w = dn_conv_w[0]
    a_col, dt_col, kind_col, dir_col = _gate_params(dn_a_log[0], dn_dt_bias[0])

    def dn_inputs(pp):
        q = _conv_act(pp, conv_w, 0, key_dim, l2norm=True, out_scale=DK ** -0.5)
        k = _conv_act(pp, conv_w, key_dim, key_dim, l2norm=True)
        v = _conv_act(pp, conv_w, 2 * key_dim, val_dim, l2norm=False)
        ba = pp[:, :, qkv_dim + val_dim:qkv_dim + val_dim + 4 * HV]
        gr = _gates(_gate_rows(ba, chunk), a_col, dt_col, kind_col, dir_col, chunk)
        return q, k, v, _chunk_major(gr, chunk)

    ql, kl, vl, grl = dn_inputs(p)
    qc, kc, vc, grc = dn_inputs(pc)
    y = _delta(qc, kc, vc, grc, ql, kl, vl, grl, p, qkv_dim, dn_norm_g[0], chunk)
    h = _matmul(y.reshape(t_lat, val_dim), dn_w_out[0].astype(BF16), out_dtype=F32, tm=1024,
                tn=512, modv=modv(0, 1), resid=h, name="dn_out_proj")
    h = _ffn(h, modv(0, 2), norm_g[0, 2], *ffn_weights(0, 1))

    h = _ffn(h, modv(1, 0), norm_g[1, 0], *ffn_weights(1, 0))
    gdim = d // N_FGROUPS
    cos_c, sin_c = _dft_tables(gdim)
    w_cs = _chan_dft_weights(jnp.stack([cos_c, sin_c]), fn_w_out[0].astype(BF16))
    ab = _matmul(h, w_cs, out_dtype=BF16, tm=1024, tn=512, modv=modv(1, 1), norm_g=norm_g[1, 1],
                 name="fourier_in").reshape(bsz, seq, 2 * d)
    cos_l, sin_l = _dft_tables(seq)
    h = _seq_dft(cos_l, sin_l, ab, modv(1, 1), h.reshape(bsz, seq, d),
                 1.0 / math.sqrt(seq * gdim)).reshape(t_lat, d)
    h = _ffn(h, modv(1, 2), norm_g[1, 2], *ffn_weights(1, 1), final_g=final_norm_g)
    return h.reshape(bsz, seq, d)
```

```python
import functools
import math

import jax
import jax.numpy as jnp
from jax import lax
from jax.experimental import pallas as pl
from jax.experimental.pallas import tpu as pltpu

F32 = jnp.float32
BF16 = jnp.bfloat16

EPS = 1e-6
FFN_RES = 0.5
N_SUB = 3
HK = 16
HV = 32
DK = 128
DV = 128
CONV_W = 5
N_FGROUPS = 4
DELTA_CHUNK = 128
PHASE_A_GROUP = 4

LANES = 128
MIB = 1024 * 1024


def _cparams(semantics, vmem_mib):
    return pltpu.CompilerParams(dimension_semantics=semantics,
                                vmem_limit_bytes=int(vmem_mib * MIB))


def _dot(a, b):
    return jnp.dot(a, b, preferred_element_type=F32)


def _dot_nt(a, b):
    return lax.dot_general(a, b, (((1,), (1,)), ((), ())), preferred_element_type=F32)


def _dot_tn(a, b):
    return lax.dot_general(a, b, (((0,), (0,)), ((), ())), preferred_element_type=F32)


def _silu(x):
    return x * jax.nn.sigmoid(x)


def _ada_norm(x, g, shift, scale):
    ms = jnp.mean(x * x, axis=-1, keepdims=True)
    return (x * lax.rsqrt(ms + EPS) * g) * (1.0 + scale) + shift


def _mod_kernel(cond_ref, w_ref, b_ref, o_ref):
    s = _silu(cond_ref[...]).astype(BF16)
    o_ref[0] = _dot(s, w_ref[0].astype(BF16)) + b_ref[0]


def _modulation(cond, mod_w, mod_b):
    depth, d, n = mod_w.shape
    tn = 1024
    return pl.pallas_call(
        _mod_kernel,
        out_shape=jax.ShapeDtypeStruct((depth, 8, n), F32),
        grid=(depth, n // tn),
        in_specs=[pl.BlockSpec((8, d), lambda l, j: (0, 0)),
                  pl.BlockSpec((1, d, tn), lambda l, j: (l, 0, j)),
                  pl.BlockSpec((1, 1, tn), lambda l, j: (l, 0, j))],
        out_specs=pl.BlockSpec((1, 8, tn), lambda l, j: (l, 0, j)),
        compiler_params=_cparams(("parallel", "parallel"), 40),
        name="modulation",
    )(cond, mod_w, mod_b.reshape(depth, 1, n))


def _ffn_kernel(x_ref, mod_ref, g_ref, wg_ref, wu_ref, wd_ref, fg_ref, o_ref, h_ref, *,
                final_norm):
    j = pl.program_id(1)
    nj = pl.num_programs(1)

    @pl.when(j == 0)
    def _():
        h_ref[...] = _ada_norm(x_ref[...], g_ref[...], mod_ref[0, 0:1, :],
                               mod_ref[0, 1:2, :]).astype(BF16)
        o_ref[...] = jnp.zeros_like(o_ref)

    h = h_ref[...]
    a = (_silu(_dot(h, wg_ref[...])) * _dot(h, wu_ref[...])).astype(BF16)
    o_ref[...] += _dot(a, wd_ref[...])

    @pl.when(j == nj - 1)
    def _():
        r = x_ref[...] + (FFN_RES * mod_ref[0, 2:3, :]) * o_ref[...]
        if final_norm:
            ms = jnp.mean(r * r, axis=-1, keepdims=True)
            r = r * lax.rsqrt(ms + EPS) * fg_ref[...]
        o_ref[...] = r


def _ffn(x, modv, g, wg, wu, wd, final_g=None):
    t, d = x.shape
    f = wg.shape[1]
    tm = min(512, t)
    tf = 512
    tiles_per_mod = t // tm // modv.shape[0]
    fg = (final_g if final_g is not None else g).reshape(1, d)
    return pl.pallas_call(
        functools.partial(_ffn_kernel, final_norm=final_g is not None),
        out_shape=jax.ShapeDtypeStruct((t, d), F32),
        grid=(t // tm, f // tf),
        in_specs=[pl.BlockSpec((tm, d), lambda i, j: (i, 0)),
                  pl.BlockSpec((1, 8, d), lambda i, j: (i // tiles_per_mod, 0, 0)),
                  pl.BlockSpec((1, d), lambda i, j: (0, 0)),
                  pl.BlockSpec((d, tf), lambda i, j: (0, j)),
                  pl.BlockSpec((d, tf), lambda i, j: (0, j)),
                  pl.BlockSpec((tf, d), lambda i, j: (j, 0)),
                  pl.BlockSpec((1, d), lambda i, j: (0, 0))],
        out_specs=pl.BlockSpec((tm, d), lambda i, j: (i, 0)),
        scratch_shapes=[pltpu.VMEM((tm, d), BF16)],
        compiler_params=_cparams(("parallel", "arbitrary"), 48),
        name="ffn_half",
    )(x, modv, g.reshape(1, d), wg, wu, wd, fg)


def _mm_kernel(*refs, norm, resid):
    it = iter(refs)
    a_ref = next(it)
    b_ref = next(it)
    if norm:
        mod_ref, g_ref = next(it), next(it)
    elif resid:
        mod_ref = next(it)
    if resid:
        r_ref = next(it)
    o_ref = next(it)

    if norm:
        h_ref = next(it)

        @pl.when(pl.program_id(1) == 0)
        def _():
            h_ref[...] = _ada_norm(a_ref[...], g_ref[...], mod_ref[0, 0:1, :],
                                   mod_ref[0, 1:2, :]).astype(BF16)
        a = h_ref[...]
    else:
        a = a_ref[...]
    acc = _dot(a, b_ref[...])
    if resid:
        acc = r_ref[...] + mod_ref[0, 2:3, :] * acc
    o_ref[...] = acc.astype(o_ref.dtype)


def _matmul(a, b, *, out_dtype, tm, tn, modv=None, norm_g=None, resid=None, vmem_mib=48,
            name="matmul"):
    m, k = a.shape
    n = b.shape[1]
    tm = min(tm, m)
    norm = norm_g is not None
    has_res = resid is not None
    in_specs = [pl.BlockSpec((tm, k), lambda i, j: (i, 0)),
                pl.BlockSpec((k, tn), lambda i, j: (0, j))]
    args = [a, b]
    if norm or has_res:
        tiles_per_mod = m // tm // modv.shape[0]
        if norm:
            in_specs.append(pl.BlockSpec((1, 8, k), lambda i, j: (i // tiles_per_mod, 0, 0)))
            in_specs.append(pl.BlockSpec((1, k), lambda i, j: (0, 0)))
            args += [modv, norm_g.reshape(1, k)]
        else:
            in_specs.append(pl.BlockSpec((1, 8, tn), lambda i, j: (i // tiles_per_mod, 0, j)))
            args.append(modv)
    if has_res:
        in_specs.append(pl.BlockSpec((tm, tn), lambda i, j: (i, j)))
        args.append(resid)
    return pl.pallas_call(
        functools.partial(_mm_kernel, norm=norm, resid=has_res),
        out_shape=jax.ShapeDtypeStruct((m, n), out_dtype),
        grid=(m // tm, n // tn),
        in_specs=in_specs,
        out_specs=pl.BlockSpec((tm, tn), lambda i, j: (i, j)),
        scratch_shapes=[pltpu.VMEM((tm, k), BF16)] if norm else [],
        compiler_params=_cparams(("parallel", "arbitrary"), vmem_mib),
        name=name,
    )(*args)


_CONV_ROWS = 256
_CONV_HALO = 8


def _conv_kernel(x_ref, w_ref, o_ref, xp_ref, *, seq, tc, l2norm, out_scale):
    zeros = jnp.zeros((_CONV_HALO, tc), F32)
    xp_ref[0:_CONV_HALO, :] = zeros
    xp_ref[_CONV_HALO + seq:, :] = zeros
    xp_ref[_CONV_HALO:_CONV_HALO + seq, :] = x_ref[0]
    w = w_ref[...]
    rows = min(_CONV_ROWS, seq)

    def step(t):
        base = t * rows
        acc = None
        for j in range(CONV_W):
            start = base + _CONV_HALO - CONV_W // 2 + j
            term = xp_ref[start:start + rows, :] * w[j:j + 1, :]
            acc = term if acc is None else acc + term
        y = _silu(acc)
        if l2norm:
            parts = []
            for hh in range(tc // LANES):
                yh = y[:, hh * LANES:(hh + 1) * LANES]
                ss = jnp.sum(yh * yh, axis=-1, keepdims=True)
                parts.append(yh * (lax.rsqrt(ss + EPS) * out_scale))
            y = parts[0] if len(parts) == 1 else jnp.concatenate(parts, axis=-1)
        o_ref[0, base:base + rows, :] = y.astype(o_ref.dtype)

    for t in range(seq // rows):
        step(t)


def _conv_act(p, conv_w, col0, ncols, *, l2norm, out_scale=1.0):
    b, s, _ = p.shape
    tc = 256
    off = col0 // tc
    return pl.pallas_call(
        functools.partial(_conv_kernel, seq=s, tc=tc, l2norm=l2norm, out_scale=out_scale),
        out_shape=jax.ShapeDtypeStruct((b, s, ncols), BF16),
        grid=(b, ncols // tc),
        in_specs=[pl.BlockSpec((1, s, tc), lambda i, j: (i, 0, j + off)),
                  pl.BlockSpec((CONV_W, tc), lambda i, j: (0, j + off))],
        out_specs=pl.BlockSpec((1, s, tc), lambda i, j: (i, 0, j)),
        scratch_shapes=[pltpu.VMEM((s + 2 * _CONV_HALO, tc), F32)],
        compiler_params=_cparams(("parallel", "parallel"), 40),
        name="dn_conv",
    )(p, conv_w)


def _split3(x):
    x1 = x.astype(BF16)
    r1 = x - x1.astype(F32)
    x2 = r1.astype(BF16)
    x3 = (r1 - x2.astype(F32)).astype(BF16)
    return x1, x2, x3


def _gates_kernel(ba_ref, a_ref, dt_ref, kind_ref, dir_ref, o_ref, *, chunk, tl):
    x = ba_ref[0]
    beta = jax.nn.sigmoid(x)
    y = x + dt_ref[...]
    softplus = jnp.maximum(y, 0.0) + jnp.log(1.0 + jnp.exp(-jnp.abs(y)))
    g = -jnp.exp(a_ref[...]) * softplus
    i = lax.broadcasted_iota(jnp.int32, (tl, tl), 0)
    j = lax.broadcasted_iota(jnp.int32, (tl, tl), 1)
    sh = int(math.log2(chunk))
    same = lax.shift_right_logical(i, sh) == lax.shift_right_logical(j, sh)
    pre = jnp.where(same & (i <= j), 1.0, 0.0).astype(BF16)
    suf = jnp.where(same & (i >= j), 1.0, 0.0).astype(BF16)
    g1, g2, g3 = _split3(g)
    cpre = _dot(g1, pre) + _dot(g2, pre) + _dot(g3, pre)
    csuf = _dot(g1, suf) + _dot(g2, suf) + _dot(g3, suf)
    cum = jnp.where(dir_ref[...] > 0.5, csuf, cpre)
    o_ref[0] = jnp.where(kind_ref[...] > 0.5, cum, beta)


def _gates(ba_t, a_col, dt_col, kind_col, dir_col, chunk):
    b, r, s = ba_t.shape
    tl = min(512, s)
    col = pl.BlockSpec((r, 1), lambda i, j: (0, 0))
    return pl.pallas_call(
        functools.partial(_gates_kernel, chunk=chunk, tl=tl),
        out_shape=jax.ShapeDtypeStruct((b, r, s), F32),
        grid=(b, s // tl),
        in_specs=[pl.BlockSpec((1, r, tl), lambda i, j: (i, 0, j)), col, col, col, col],
        out_specs=pl.BlockSpec((1, r, tl), lambda i, j: (i, 0, j)),
        compiler_params=_cparams(("parallel", "parallel"), 32),
        name="dn_gates",
    )(ba_t, a_col, dt_col, kind_col, dir_col)


def _col_form(gr_ref, dst_ref, n_rows, chunk):
    pad = jnp.zeros((LANES - 8, LANES), F32)
    for s in range(n_rows // LANES):
        if chunk <= LANES:
            per = LANES // chunk
            parts = [gr_ref[0, 0, s * per + t] for t in range(per)]
            slab = parts[0] if per == 1 else jnp.concatenate(parts, axis=-1)
        else:
            per = chunk // LANES
            slab = gr_ref[0, 0, s // per, :, (s % per) * LANES:(s % per + 1) * LANES]
        dst_ref[s * LANES:(s + 1) * LANES, :] = jnp.concatenate([slab, pad], axis=0).T


def _delta_kernel(qc_ref, kc_ref, vc_ref, grc_ref, ql_ref, kl_ref, vl_ref, grl_ref,
                  z_ref, ng_ref, o_ref, colc_ref, coll_ref, aq_ref, b_ref, gl_ref, s_ref,
                  oacc_ref, *, chunk, seq_c, seq_l):
    C = chunk
    n_c, n_l = seq_c // C, seq_l // C
    _col_form(grc_ref, colc_ref, seq_c, C)
    _col_form(grl_ref, coll_ref, seq_l, C)
    s_ref[...] = jnp.zeros_like(s_ref)
    oacc_ref[...] = jnp.zeros_like(oacc_ref)

    row = lax.broadcasted_iota(jnp.int32, (C, C), 0)
    col = lax.broadcasted_iota(jnp.int32, (C, C), 1)
    eye = jnp.where(row == col, 1.0, 0.0)

    def same_block(n):
        return lax.shift_right_logical(row, n) == lax.shift_right_logical(col, n)

    pair_mask = jnp.where(same_block(1), 1.0, 0.0)
    join_masks = [jnp.where(same_block(n + 1), 1.0, 0.0) - jnp.where(same_block(n), 1.0, 0.0)
                  for n in range(1, int(math.log2(C)))]

    chains = range(4)

    def phase_a(q_ref, k_ref, v_ref, gr_ref, col_ref, slot0, want_out, group):
        items = [(g, c) for g in range(group) for c in chains]

        def body(it, carry):
            cis = [it * group + g for g in range(group)]
            r0s = [pl.multiple_of(ci * C, C) for ci in cis]
            kb = [k_ref[0, pl.ds(r0, C), :] for r0 in r0s]
            kf = [t.astype(F32) for t in kb]
            gram = [_dot_nt(t, t) for t in kb]
            if want_out:
                qb = [q_ref[0, pl.ds(r0, C), :] for r0 in r0s]
                qk = [_dot_nt(qb[g], kb[g]) for g in range(group)]
            cols = [col_ref[pl.ds(r0, C), :] for r0 in r0s]
            beta, gcol, decay, lm, x = [], [], [], [], []
            for n, (g, c) in enumerate(items):
                d = c // 2
                beta.append(cols[g][:, c:c + 1])
                gcol.append(cols[g][:, 4 + c:5 + c])
                grow = gr_ref[0, 0, cis[g], 4 + c:5 + c, :]
                incl = (row >= col) if d == 0 else (row <= col)
                strict = (row > col) if d == 0 else (row < col)
                decay.append(jnp.exp(jnp.where(incl, gcol[n] - grow, -jnp.inf)))
                lm.append(jnp.where(strict, gram[g] * decay[n] * beta[n], 0.0))
                x.append(eye - lm[n] * pair_mask)
            idx = range(len(items))
            for jm in join_masks:
                y = [_dot((lm[n] * jm).astype(BF16), x[n].astype(BF16)) for n in idx]
                x = [x[n] - _dot(x[n].astype(BF16), y[n].astype(BF16)) for n in idx]
            eg = [jnp.exp(gcol[n]) for n in idx]
            sol = []
            for n, (g, c) in enumerate(items):
                e = c % 2
                vf = v_ref[0, pl.ds(r0s[g], C), e * DV:(e + 1) * DV].astype(F32)
                rhs = jnp.concatenate([vf * beta[n], kf[g] * (beta[n] * eg[n])], axis=-1)
                sol.append(_dot(x[n].astype(BF16), rhs.astype(BF16)).astype(BF16))
            kw = []
            for n, (g, c) in enumerate(items):
                glast = gcol[n][C - 1:C, :] if c // 2 == 0 else gcol[n][0:1, :]
                ktail = (kf[g] * jnp.exp(glast - gcol[n])).astype(BF16)
                kw.append(_dot_tn(ktail, sol[n]))
                gl_ref[c, slot0 + cis[g]] = jnp.broadcast_to(jnp.exp(glast), (8, LANES))
            if want_out:
                aw = [_dot((qk[g] * decay[n]).astype(BF16), sol[n])
                      for n, (g, c) in enumerate(items)]
            for n, (g, c) in enumerate(items):
                slot = slot0 + cis[g]
                b_ref[c, slot] = kw[n][:, :DV]
                aq_ref[c, slot, 0:DK, :] = (-kw[n][:, DV:]).astype(BF16)
                if want_out:
                    aq_ref[c, slot, DK:DK + C, :] = (qb[g].astype(F32) * eg[n]
                                                     - aw[n][:, DV:]).astype(BF16)
            if want_out:
                for g in range(group):
                    for e in range(2):
                        oacc_ref[pl.ds(r0s[g], C), e * DV:(e + 1) * DV] += (
                            aw[4 * g + e][:, :DV] + aw[4 * g + 2 + e][:, :DV])
            return carry
        return body

    def phase_b(n, slot0, want_out):
        def body(i, carry):
            states = [s_ref[c] for c in chains]
            cis = [i if c // 2 == 0 else n - 1 - i for c in chains]
            res = []
            for c in chains:
                slot = slot0 + cis[c]
                lhs = aq_ref[c, slot] if want_out else aq_ref[c, slot, 0:DK, :]
                res.append(_dot(lhs, states[c].astype(BF16)))
            for c in chains:
                slot = slot0 + cis[c]
                s_ref[c] = states[c] * gl_ref[c, slot, 0:1, :] + res[c][:DK] + b_ref[c, slot]
                if want_out:
                    r0 = pl.multiple_of(cis[c] * C, C)
                    e = c % 2
                    oacc_ref[pl.ds(r0, C), e * DV:(e + 1) * DV] += res[c][DK:]
            return carry
        return body

    grp_c = math.gcd(n_c, PHASE_A_GROUP)
    grp_l = math.gcd(n_l, PHASE_A_GROUP)
    lax.fori_loop(0, n_c // grp_c,
                  phase_a(qc_ref, kc_ref, vc_ref, grc_ref, colc_ref, 0, False, grp_c), 0)
    lax.fori_loop(0, n_l // grp_l,
                  phase_a(ql_ref, kl_ref, vl_ref, grl_ref, coll_ref, n_c, True, grp_l), 0)
    lax.fori_loop(0, n_c, phase_b(n_c, 0, False), 0)
    lax.fori_loop(0, n_l, phase_b(n_l, n_c, True), 0)

    rows = min(512, seq_l)

    def epilogue(t, carry):
        base = pl.multiple_of(t * rows, 8)
        for e in range(2):
            o = oacc_ref[pl.ds(base, rows), e * DV:(e + 1) * DV]
            ms = jnp.mean(o * o, axis=-1, keepdims=True)
            y = o * lax.rsqrt(ms + EPS) * ng_ref[...]
            zz = z_ref[0, pl.ds(base, rows), e * DV:(e + 1) * DV]
            o_ref[0, pl.ds(base, rows), e * DV:(e + 1) * DV] = (y * _silu(zz)).astype(o_ref.dtype)
        return carry

    lax.fori_loop(0, seq_l // rows, epilogue, 0)


def _delta(qc, kc, vc, grc, ql, kl, vl, grl, p_lat, z_col0, norm_g, chunk):
    b, seq_c, _ = qc.shape
    seq_l = ql.shape[1]
    n_c, n_l = seq_c // chunk, seq_l // chunk
    zoff = z_col0 // (2 * DV)

    def tok(seq, width):
        return pl.BlockSpec((1, seq, width), lambda i, h: (i, 0, h))

    def gate_spec(n):
        return pl.BlockSpec((1, 1, n, 8, chunk), lambda i, h: (i, h, 0, 0, 0))

    return pl.pallas_call(
        functools.partial(_delta_kernel, chunk=chunk, seq_c=seq_c, seq_l=seq_l),
        out_shape=jax.ShapeDtypeStruct((b, seq_l, HV * DV), BF16),
        grid=(b, HK),
        in_specs=[tok(seq_c, DK), tok(seq_c, DK), tok(seq_c, 2 * DV), gate_spec(n_c),
                  tok(seq_l, DK), tok(seq_l, DK), tok(seq_l, 2 * DV), gate_spec(n_l),
                  pl.BlockSpec((1, seq_l, 2 * DV), lambda i, h: (i, 0, h + zoff)),
                  pl.BlockSpec((1, DV), lambda i, h: (0, 0))],
        out_specs=tok(seq_l, 2 * DV),
        scratch_shapes=[pltpu.VMEM((seq_c, LANES), F32),
                        pltpu.VMEM((seq_l, LANES), F32),
                        pltpu.VMEM((4, n_c + n_l, DK + chunk, DV), BF16),
                        pltpu.VMEM((4, n_c + n_l, DK, DV), F32),
                        pltpu.VMEM((4, n_c + n_l, 8, LANES), F32),
                        pltpu.VMEM((4, DK, DV), F32),
                        pltpu.VMEM((seq_l, 2 * DV), F32)],
        compiler_params=_cparams(("parallel", "parallel"), 56),
        name="dn_delta",
    )(qc, kc, vc, grc, ql, kl, vl, grl, p_lat, norm_g.reshape(1, DV))


def _chan_dft_kernel(tab_ref, w_ref, o_ref):
    o_ref[...] = _dot(tab_ref[0], w_ref[...]).astype(o_ref.dtype)


def _chan_dft_weights(tabs, w):
    d = w.shape[0]
    g = tabs.shape[1]
    tn = 512
    nj = d // tn
    return pl.pallas_call(
        _chan_dft_kernel,
        out_shape=jax.ShapeDtypeStruct((d, 2 * d), BF16),
        grid=(d // g, 2, nj),
        in_specs=[pl.BlockSpec((1, g, g), lambda i, t, j: (t, 0, 0)),
                  pl.BlockSpec((g, tn), lambda i, t, j: (i, j))],
        out_specs=pl.BlockSpec((g, tn), lambda i, t, j: (i, t * nj + j)),
        compiler_params=_cparams(("parallel", "parallel", "parallel"), 32),
        name="fourier_chan",
    )(tabs, w)


def _seq_dft_kernel(cos_ref, sin_ref, a_ref, b_ref, mod_ref, r_ref, o_ref, *, scale):
    acc = _dot(cos_ref[...], a_ref[0]) - _dot(sin_ref[...], b_ref[0])
    o_ref[0] = r_ref[0] + (mod_ref[0, 2:3, :] * scale) * acc


def _seq_dft(cos_t, sin_t, ab, modv, resid, scale):
    b, seq, d2 = ab.shape
    d = d2 // 2
    tm, tn = 512, 512
    nj = d // tn
    return pl.pallas_call(
        functools.partial(_seq_dft_kernel, scale=scale),
        out_shape=jax.ShapeDtypeStruct((b, seq, d), F32),
        grid=(b, seq // tm, nj),
        in_specs=[pl.BlockSpec((tm, seq), lambda n, i, j: (i, 0)),
                  pl.BlockSpec((tm, seq), lambda n, i, j: (i, 0)),
                  pl.BlockSpec((1, seq, tn), lambda n, i, j: (n, 0, j)),
                  pl.BlockSpec((1, seq, tn), lambda n, i, j: (n, 0, j + nj)),
                  pl.BlockSpec((1, 8, tn), lambda n, i, j: (n, 0, j)),
                  pl.BlockSpec((1, tm, tn), lambda n, i, j: (n, i, j))],
        out_specs=pl.BlockSpec((1, tm, tn), lambda n, i, j: (n, i, j)),
        compiler_params=_cparams(("parallel", "parallel", "arbitrary"), 48),
        name="fourier_seq",
    )(cos_t, sin_t, ab, ab, modv, resid)


def _dft_table_kernel(cos_ref, sin_ref, tc_ref, ts_ref, *, n, tk):
    w = 2.0 * math.pi / n

    def unit(phase):
        ang = (phase & (n - 1)).astype(F32) * w
        return jnp.cos(ang), jnp.sin(ang)

    @pl.when(pl.program_id(0) == 0)
    def _():
        a = lax.broadcasted_iota(jnp.int32, (tk, n), 0)
        m = lax.broadcasted_iota(jnp.int32, (tk, n), 1)
        tc_ref[...], ts_ref[...] = unit(a * m)

    m1 = lax.broadcasted_iota(jnp.int32, (1, n), 1)
    rc, rs = unit((pl.program_id(0) * tk) * m1)
    tcv, tsv = tc_ref[...], ts_ref[...]
    cos_ref[...] = (rc * tcv - rs * tsv).astype(cos_ref.dtype)
    sin_ref[...] = (rs * tcv + rc * tsv).astype(sin_ref.dtype)


def _dft_tables(n):
    tk = min(256, n)
    out = jax.ShapeDtypeStruct((n, n), BF16)
    spec = pl.BlockSpec((tk, n), lambda i: (i, 0))
    return pl.pallas_call(
        functools.partial(_dft_table_kernel, n=n, tk=tk),
        out_shape=(out, out),
        grid=(n // tk,),
        in_specs=[],
        out_specs=(spec, spec),
        scratch_shapes=[pltpu.VMEM((tk, n), F32), pltpu.VMEM((tk, n), F32)],
        compiler_params=_cparams(("arbitrary",), 40),
        name="dft_tables",
    )()


def _gate_rows(p_ba, chunk):
    b, s, _ = p_ba.shape
    t = p_ba.reshape(b, s, 2, 2, HK, 2)
    return jnp.transpose(t, (0, 4, 2, 3, 5, 1)).reshape(b, 4 * HV, s)


def _gate_params(a_log, dt_bias):
    def rows(v, kind_fill):
        t = jnp.stack([jnp.full_like(v, kind_fill), v]).reshape(2, 2, HK, 2)
        return jnp.transpose(t, (2, 0, 1, 3)).reshape(4 * HV, 1)
    kind = rows(jnp.ones((2, HV), F32), 0.0)
    dirn = rows(jnp.stack([jnp.zeros((HV,), F32), jnp.ones((HV,), F32)]), 0.0)
    return rows(a_log.astype(F32), 0.0), rows(dt_bias.astype(F32), 0.0), kind, dirn


def _chunk_major(rows_t, chunk):
    b, _, s = rows_t.shape
    t = rows_t.reshape(b, HK, 8, s // chunk, chunk)
    return jnp.transpose(t, (0, 1, 3, 2, 4))


def kernel(x, c, ctx, c_ctx, norm_g, mod_w, mod_b, ffn_w_gate, ffn_w_up, ffn_w_down, dn_w_in,
           dn_conv_w, dn_a_log, dn_dt_bias, dn_norm_g, dn_w_out, fn_w_out, final_norm_g):
    bsz, seq, d = x.shape
    seq_c = ctx.shape[1]
    t_lat, t_ctx = bsz * seq, bsz * seq_c
    key_dim, val_dim = HK * DK, HV * DV
    qkv_dim = 2 * key_dim + val_dim
    chunk = DELTA_CHUNK

    cond = jnp.zeros((8, d), F32).at[:bsz].set(c).at[bsz].set(c_ctx)
    mod = _modulation(cond, mod_w, mod_b)

    def modv(layer, sub, ctx_rows=False):
        m3 = mod[layer, :, 3 * sub * d:3 * (sub + 1) * d].reshape(8, 3, d)
        m3 = m3[bsz:bsz + 1] if ctx_rows else m3[:bsz]
        return jnp.pad(m3, ((0, 0), (0, 5), (0, 0)))

    def ffn_weights(layer, which):
        return (ffn_w_gate[layer, which].astype(BF16), ffn_w_up[layer, which].astype(BF16),
                ffn_w_down[layer, which].astype(BF16))

    h = x.reshape(t_lat, d)
    hc = ctx.reshape(t_ctx, d)

    wa = ffn_weights(0, 0)
    h = _ffn(h, modv(0, 0), norm_g[0, 0], *wa)
    hc = _ffn(hc, modv(0, 0, True), norm_g[0, 0], *wa)

    n_in = dn_w_in.shape[2]
    n_pad = -(-n_in // 512) * 512
    w_in = jnp.pad(dn_w_in[0].astype(BF16), ((0, 0), (0, n_pad - n_in)))
    p = _matmul(h, w_in, out_dtype=F32, tm=1024, tn=512, modv=modv(0, 1), norm_g=norm_g[0, 1],
                name="dn_in_proj").reshape(bsz, seq, n_pad)
    pc = _matmul(hc, w_in, out_dtype=F32, tm=512, tn=512, modv=modv(0, 1, True),
                 norm_g=norm_g[0, 1], name="dn_in_proj_ctx").reshape(bsz, seq_c, n_pad)

    conv_w = dn_conv_w[0]
    a_col, dt_col, kind_col, dir_col = _gate_params(dn_a_log[0], dn_dt_bias[0])

    def dn_inputs(pp):
        q = _conv_act(pp, conv_w, 0, key_dim, l2norm=True, out_scale=DK ** -0.5)
        k = _conv_act(pp, conv_w, key_dim, key_dim, l2norm=True)
        v = _conv_act(pp, conv_w, 2 * key_dim, val_dim, l2norm=False)
        ba = pp[:, :, qkv_dim + val_dim:qkv_dim + val_dim + 4 * HV]
        gr = _gates(_gate_rows(ba, chunk), a_col, dt_col, kind_col, dir_col, chunk)
        return q, k, v, _chunk_major(gr, chunk)

    ql, kl, vl, grl = dn_inputs(p)
    qc, kc, vc, grc = dn_inputs(pc)
    y = _delta(qc, kc, vc, grc, ql, kl, vl, grl, p, qkv_dim, dn_norm_g[0], chunk)
    h = _matmul(y.reshape(t_lat, val_dim), dn_w_out[0].astype(BF16), out_dtype=F32, tm=1024,
                tn=512, modv=modv(0, 1), resid=h, name="dn_out_proj")
    h = _ffn(h, modv(0, 2), norm_g[0, 2], *ffn_weights(0, 1))

    h = _ffn(h, modv(1, 0), norm_g[1, 0], *ffn_weights(1, 0))
    gdim = d // N_FGROUPS
    cos_c, sin_c = _dft_tables(gdim)
    w_cs = _chan_dft_weights(jnp.stack([cos_c, sin_c]), fn_w_out[0].astype(BF16))
    ab = _matmul(h, w_cs, out_dtype=BF16, tm=1024, tn=512, modv=modv(1, 1), norm_g=norm_g[1, 1],
                 name="fourier_in").reshape(bsz, seq, 2 * d)
    cos_l, sin_l = _dft_tables(seq)
    h = _seq_dft(cos_l, sin_l, ab, modv(1, 1), h.reshape(bsz, seq, d),
                 1.0 / math.sqrt(seq * gdim)).reshape(t_lat, d)
    h = _ffn(h, modv(1, 2), norm_g[1, 2], *ffn_weights(1, 1), final_g=final_norm_g)
    return h.reshape(bsz, seq, d)
```

```python
import functools
import math

import jax
import jax.numpy as jnp
from jax import lax
from jax.experimental import pallas as pl
from jax.experimental.pallas import tpu as pltpu

F32 = jnp.float32
BF16 = jnp.bfloat16

EPS = 1e-6
FFN_RES = 0.5
N_SUB = 3
HK = 16
HV = 32
DK = 128
DV = 128
CONV_W = 5
N_FGROUPS = 4
DELTA_CHUNK = 128
PHASE_A_GROUP = 4

LANES = 128
MIB = 1024 * 1024


def _cparams(semantics, vmem_mib):
    return pltpu.CompilerParams(dimension_semantics=semantics,
                                vmem_limit_bytes=int(vmem_mib * MIB))


def _dot(a, b):
    return jnp.dot(a, b, preferred_element_type=F32)


def _dot_nt(a, b):
    return lax.dot_general(a, b, (((1,), (1,)), ((), ())), preferred_element_type=F32)


def _dot_tn(a, b):
    return lax.dot_general(a, b, (((0,), (0,)), ((), ())), preferred_element_type=F32)


def _silu(x):
    return x * jax.nn.sigmoid(x)


_NORM_ROWS = 32


def _ada_norm_into(h_ref, x_ref, g_ref, mod_ref):
    gain = g_ref[...] * (1.0 + mod_ref[0, 1:2, :])
    shift = mod_ref[0, 0:1, :]
    rows = min(_NORM_ROWS, x_ref.shape[0])

    def step(t, carry):
        r0 = pl.multiple_of(t * rows, rows)
        x = x_ref[pl.ds(r0, rows), :]
        ms = jnp.mean(x * x, axis=-1, keepdims=True)
        h_ref[pl.ds(r0, rows), :] = (x * lax.rsqrt(ms + EPS) * gain + shift).astype(BF16)
        return carry

    lax.fori_loop(0, x_ref.shape[0] // rows, step, 0, unroll=2)


def _mod_kernel(cond_ref, w_ref, b_ref, o_ref):
    s = _silu(cond_ref[...]).astype(BF16)
    o_ref[0] = _dot(s, w_ref[0].astype(BF16)) + b_ref[0]


def _modulation(cond, mod_w, mod_b):
    depth, d, n = mod_w.shape
    tn = 1024
    return pl.pallas_call(
        _mod_kernel,
        out_shape=jax.ShapeDtypeStruct((depth, 8, n), F32),
        grid=(depth, n // tn),
        in_specs=[pl.BlockSpec((8, d), lambda l, j: (0, 0)),
                  pl.BlockSpec((1, d, tn), lambda l, j: (l, 0, j)),
                  pl.BlockSpec((1, 1, tn), lambda l, j: (l, 0, j))],
        out_specs=pl.BlockSpec((1, 8, tn), lambda l, j: (l, 0, j)),
        compiler_params=_cparams(("parallel", "parallel"), 40),
        name="modulation",
    )(cond, mod_w, mod_b.reshape(depth, 1, n))


def _ffn_kernel(x_ref, mod_ref, g_ref, wg_ref, wu_ref, wd_ref, fg_ref, o_ref, h_ref, *,
                final_norm):
    j = pl.program_id(1)
    nj = pl.num_programs(1)

    @pl.when(j == 0)
    def _():
        _ada_norm_into(h_ref, x_ref, g_ref, mod_ref)
        o_ref[...] = jnp.zeros_like(o_ref)

    h = h_ref[...]
    a = (_silu(_dot(h, wg_ref[...])) * _dot(h, wu_ref[...])).astype(BF16)
    o_ref[...] += _dot(a, wd_ref[...])

    @pl.when(j == nj - 1)
    def _():
        r = x_ref[...] + (FFN_RES * mod_ref[0, 2:3, :]) * o_ref[...]
        if final_norm:
            ms = jnp.mean(r * r, axis=-1, keepdims=True)
            r = r * lax.rsqrt(ms + EPS) * fg_ref[...]
        o_ref[...] = r


def _ffn(x, modv, g, wg, wu, wd, layer, which, final_g=None):
    t, d = x.shape
    f = wg.shape[3]
    tm = min(512, t // modv.shape[0])
    tf = 512
    tiles_per_mod = t // tm // modv.shape[0]
    fg = (final_g if final_g is not None else g).reshape(1, d)
    return pl.pallas_call(
        functools.partial(_ffn_kernel, final_norm=final_g is not None),
        out_shape=jax.ShapeDtypeStruct((t, d), F32),
        grid=(t // tm, f // tf),
        in_specs=[pl.BlockSpec((tm, d), lambda i, j: (i, 0)),
                  pl.BlockSpec((1, 8, d), lambda i, j: (i // tiles_per_mod, 0, 0)),
                  pl.BlockSpec((1, d), lambda i, j: (0, 0)),
                  pl.BlockSpec((None, None, d, tf), lambda i, j: (layer, which, 0, j)),
                  pl.BlockSpec((None, None, d, tf), lambda i, j: (layer, which, 0, j)),
                  pl.BlockSpec((None, None, tf, d), lambda i, j: (layer, which, j, 0)),
                  pl.BlockSpec((1, d), lambda i, j: (0, 0))],
        out_specs=pl.BlockSpec((tm, d), lambda i, j: (i, 0)),
        scratch_shapes=[pltpu.VMEM((tm, d), BF16)],
        compiler_params=_cparams(("parallel", "arbitrary"), 48),
        name="ffn_half",
    )(x, modv, g.reshape(1, d), wg, wu, wd, fg)


def _mm_kernel(*refs, norm, resid):
    it = iter(refs)
    a_ref = next(it)
    b_ref = next(it)
    if norm:
        mod_ref, g_ref = next(it), next(it)
    elif resid:
        mod_ref = next(it)
    if resid:
        r_ref = next(it)
    o_ref = next(it)

    if norm:
        h_ref = next(it)

        @pl.when(pl.program_id(1) == 0)
        def _():
            _ada_norm_into(h_ref, a_ref, g_ref, mod_ref)
        a = h_ref[...]
    else:
        a = a_ref[...]
    acc = _dot(a, b_ref[...])
    if resid:
        acc = r_ref[...] + mod_ref[0, 2:3, :] * acc
    o_ref[...] = acc.astype(o_ref.dtype)


def _matmul(a, b, *, out_dtype, tm, tn, modv=None, norm_g=None, resid=None, vmem_mib=48,
            name="matmul"):
    m, k = a.shape
    n = b.shape[1]
    tm = min(tm, m if modv is None else m // modv.shape[0])
    norm = norm_g is not None
    has_res = resid is not None
    in_specs = [pl.BlockSpec((tm, k), lambda i, j: (i, 0)),
                pl.BlockSpec((k, tn), lambda i, j: (0, j))]
    args = [a, b]
    if norm or has_res:
        tiles_per_mod = m // tm // modv.shape[0]
        if norm:
            in_specs.append(pl.BlockSpec((1, 8, k), lambda i, j: (i // tiles_per_mod, 0, 0)))
            in_specs.append(pl.BlockSpec((1, k), lambda i, j: (0, 0)))
            args += [modv, norm_g.reshape(1, k)]
        else:
            in_specs.append(pl.BlockSpec((1, 8, tn), lambda i, j: (i // tiles_per_mod, 0, j)))
            args.append(modv)
    if has_res:
        in_specs.append(pl.BlockSpec((tm, tn), lambda i, j: (i, j)))
        args.append(resid)
    return pl.pallas_call(
        functools.partial(_mm_kernel, norm=norm, resid=has_res),
        out_shape=jax.ShapeDtypeStruct((m, n), out_dtype),
        grid=(m // tm, pl.cdiv(n, tn)),
        in_specs=in_specs,
        out_specs=pl.BlockSpec((tm, tn), lambda i, j: (i, j)),
        scratch_shapes=[pltpu.VMEM((tm, k), BF16)] if norm else [],
        compiler_params=_cparams(("parallel", "arbitrary"), vmem_mib),
        name=name,
    )(*args)


_CONV_ROWS = 256
_CONV_HALO = 8


def _conv_kernel(x_ref, w_ref, o_ref, xp_ref, *, seq, tc, l2norm, out_scale):
    zeros = jnp.zeros((_CONV_HALO, tc), F32)
    xp_ref[0:_CONV_HALO, :] = zeros
    xp_ref[_CONV_HALO + seq:, :] = zeros
    xp_ref[_CONV_HALO:_CONV_HALO + seq, :] = x_ref[0]
    w = w_ref[...]
    rows = min(_CONV_ROWS, seq)

    def step(t):
        base = t * rows
        acc = None
        for j in range(CONV_W):
            start = base + _CONV_HALO - CONV_W // 2 + j
            term = xp_ref[start:start + rows, :] * w[j:j + 1, :]
            acc = term if acc is None else acc + term
        y = _silu(acc)
        if l2norm:
            parts = []
            for hh in range(tc // LANES):
                yh = y[:, hh * LANES:(hh + 1) * LANES]
                ss = jnp.sum(yh * yh, axis=-1, keepdims=True)
                parts.append(yh * (lax.rsqrt(ss + EPS) * out_scale))
            y = parts[0] if len(parts) == 1 else jnp.concatenate(parts, axis=-1)
        o_ref[0, base:base + rows, :] = y.astype(o_ref.dtype)

    for t in range(seq // rows):
        step(t)


def _conv_act(p, conv_w, col0, ncols, *, l2norm, out_scale=1.0):
    b, s, _ = p.shape
    tc = 256
    off = col0 // tc
    return pl.pallas_call(
        functools.partial(_conv_kernel, seq=s, tc=tc, l2norm=l2norm, out_scale=out_scale),
        out_shape=jax.ShapeDtypeStruct((b, s, ncols), BF16),
        grid=(b, ncols // tc),
        in_specs=[pl.BlockSpec((1, s, tc), lambda i, j: (i, 0, j + off)),
                  pl.BlockSpec((CONV_W, tc), lambda i, j: (0, j + off))],
        out_specs=pl.BlockSpec((1, s, tc), lambda i, j: (i, 0, j)),
        scratch_shapes=[pltpu.VMEM((s + 2 * _CONV_HALO, tc), F32)],
        compiler_params=_cparams(("parallel", "parallel"), 40),
        name="dn_conv",
    )(p, conv_w)


def _split3(x):
    x1 = x.astype(BF16)
    r1 = x - x1.astype(F32)
    x2 = r1.astype(BF16)
    x3 = (r1 - x2.astype(F32)).astype(BF16)
    return x1, x2, x3


def _gates_kernel(ba_ref, a_ref, dt_ref, kind_ref, dir_ref, o_ref, *, chunk, tl):
    x = ba_ref[0]
    beta = jax.nn.sigmoid(x)
    y = x + dt_ref[...]
    softplus = jnp.maximum(y, 0.0) + jnp.log(1.0 + jnp.exp(-jnp.abs(y)))
    g = -jnp.exp(a_ref[...]) * softplus
    i = lax.broadcasted_iota(jnp.int32, (tl, tl), 0)
    j = lax.broadcasted_iota(jnp.int32, (tl, tl), 1)
    sh = int(math.log2(chunk))
    same = lax.shift_right_logical(i, sh) == lax.shift_right_logical(j, sh)
    pre = jnp.where(same & (i <= j), 1.0, 0.0).astype(BF16)
    suf = jnp.where(same & (i >= j), 1.0, 0.0).astype(BF16)
    g1, g2, g3 = _split3(g)
    cpre = _dot(g1, pre) + _dot(g2, pre) + _dot(g3, pre)
    csuf = _dot(g1, suf) + _dot(g2, suf) + _dot(g3, suf)
    cum = jnp.where(dir_ref[...] > 0.5, csuf, cpre)
    o_ref[0] = jnp.where(kind_ref[...] > 0.5, cum, beta)


def _gates(ba_t, a_col, dt_col, kind_col, dir_col, chunk):
    b, r, s = ba_t.shape
    tl = min(512, s)
    col = pl.BlockSpec((r, 1), lambda i, j: (0, 0))
    return pl.pallas_call(
        functools.partial(_gates_kernel, chunk=chunk, tl=tl),
        out_shape=jax.ShapeDtypeStruct((b, r, s), F32),
        grid=(b, s // tl),
        in_specs=[pl.BlockSpec((1, r, tl), lambda i, j: (i, 0, j)), col, col, col, col],
        out_specs=pl.BlockSpec((1, r, tl), lambda i, j: (i, 0, j)),
        compiler_params=_cparams(("parallel", "parallel"), 32),
        name="dn_gates",
    )(ba_t, a_col, dt_col, kind_col, dir_col)


def _col_form(gr_ref, dst_ref, n_rows, chunk):
    pad = jnp.zeros((LANES - 8, LANES), F32)
    for s in range(n_rows // LANES):
        if chunk <= LANES:
            per = LANES // chunk
            parts = [gr_ref[0, 0, s * per + t] for t in range(per)]
            slab = parts[0] if per == 1 else jnp.concatenate(parts, axis=-1)
        else:
            per = chunk // LANES
            slab = gr_ref[0, 0, s // per, :, (s % per) * LANES:(s % per + 1) * LANES]
        dst_ref[s * LANES:(s + 1) * LANES, :] = jnp.concatenate([slab, pad], axis=0).T


def _delta_kernel(qc_ref, kc_ref, vc_ref, grc_ref, ql_ref, kl_ref, vl_ref, grl_ref,
                  z_ref, ng_ref, o_ref, colc_ref, coll_ref, aq_ref, b_ref, gl_ref, s_ref,
                  oacc_ref, *, chunk, seq_c, seq_l):
    C = chunk
    n_c, n_l = seq_c // C, seq_l // C
    _col_form(grc_ref, colc_ref, seq_c, C)
    _col_form(grl_ref, coll_ref, seq_l, C)
    s_ref[...] = jnp.zeros_like(s_ref)
    oacc_ref[...] = jnp.zeros_like(oacc_ref)

    row = lax.broadcasted_iota(jnp.int32, (C, C), 0)
    col = lax.broadcasted_iota(jnp.int32, (C, C), 1)
    eye = jnp.where(row == col, 1.0, 0.0)

    def same_block(n):
        return lax.shift_right_logical(row, n) == lax.shift_right_logical(col, n)

    pair_mask = jnp.where(same_block(1), 1.0, 0.0)
    join_masks = [jnp.where(same_block(n + 1), 1.0, 0.0) - jnp.where(same_block(n), 1.0, 0.0)
                  for n in range(1, int(math.log2(C)))]

    chains = range(4)

    def take_rows(a, blk, d):
        off = blk if d == 0 else 0
        return jnp.concatenate([a[s + off:s + off + blk] for s in range(0, C, 2 * blk)], axis=0)

    def put_rows(a, new, blk, d):
        parts = []
        for i, s in enumerate(range(0, C, 2 * blk)):
            upd = new[i * blk:(i + 1) * blk]
            parts += [a[s:s + blk], upd] if d == 0 else [upd, a[s + blk:s + 2 * blk]]
        return jnp.concatenate(parts, axis=0)

    def phase_a(q_ref, k_ref, v_ref, gr_ref, col_ref, slot0, want_out, group):
        items = [(g, c) for g in range(group) for c in chains]

        def body(it, carry):
            cis = [it * group + g for g in range(group)]
            r0s = [pl.multiple_of(ci * C, C) for ci in cis]
            kb = [k_ref[0, pl.ds(r0, C), :] for r0 in r0s]
            kf = [t.astype(F32) for t in kb]
            gram = [_dot_nt(t, t) for t in kb]
            if want_out:
                qb = [q_ref[0, pl.ds(r0, C), :] for r0 in r0s]
                qk = [_dot_nt(qb[g], kb[g]) for g in range(group)]
            cols = [col_ref[pl.ds(r0, C), :] for r0 in r0s]
            beta, gcol, decay, lm, x = [], [], [], [], []
            for n, (g, c) in enumerate(items):
                d = c // 2
                beta.append(cols[g][:, c:c + 1])
                gcol.append(cols[g][:, 4 + c:5 + c])
                grow = gr_ref[0, 0, cis[g], 4 + c:5 + c, :]
                incl = (row >= col) if d == 0 else (row <= col)
                strict = (row > col) if d == 0 else (row < col)
                decay.append(jnp.exp(jnp.where(incl, gcol[n] - grow, -jnp.inf)))
                lm.append(jnp.where(strict, gram[g] * decay[n] * beta[n], 0.0))
                x.append(eye - lm[n] * pair_mask)
            idx = range(len(items))
            for lvl, jm in enumerate(join_masks, start=1):
                blk = 1 << lvl
                if blk < 8:
                    y = [_dot((lm[n] * jm).astype(BF16), x[n].astype(BF16)) for n in idx]
                    x = [x[n] - _dot(x[n].astype(BF16), y[n].astype(BF16)) for n in idx]
                    continue
                jmh = [take_rows(jm, blk, d) for d in range(2)]
                zeros = jnp.zeros((blk, C), F32)
                y = [_dot((take_rows(lm[n], blk, items[n][1] // 2)
                           * jmh[items[n][1] // 2]).astype(BF16), x[n].astype(BF16)) for n in idx]
                new = []
                for n in idx:
                    d = items[n][1] // 2
                    yfull = jnp.concatenate(
                        [t for s in range(C // (2 * blk))
                         for t in ((zeros, y[n][s * blk:(s + 1) * blk]) if d == 0
                                   else (y[n][s * blk:(s + 1) * blk], zeros))], axis=0)
                    xa = take_rows(x[n], blk, d)
                    new.append(xa - _dot(xa.astype(BF16), yfull.astype(BF16)))
                x = [put_rows(x[n], new[n], blk, items[n][1] // 2) for n in idx]
            eg = [jnp.exp(gcol[n]) for n in idx]
            sol = []
            for n, (g, c) in enumerate(items):
                e = c % 2
                vf = v_ref[0, pl.ds(r0s[g], C), e * DV:(e + 1) * DV].astype(F32)
                rhs = jnp.concatenate([vf * beta[n], kf[g] * (beta[n] * eg[n])], axis=-1)
                sol.append(_dot(x[n].astype(BF16), rhs.astype(BF16)).astype(BF16))
            kw = []
            for n, (g, c) in enumerate(items):
                glast = gcol[n][C - 1:C, :] if c // 2 == 0 else gcol[n][0:1, :]
                ktail = (kf[g] * jnp.exp(glast - gcol[n])).astype(BF16)
                kw.append(_dot_tn(ktail, sol[n]))
                gl_ref[c, slot0 + cis[g]] = jnp.broadcast_to(jnp.exp(glast), (8, LANES))
            if want_out:
                aw = [_dot((qk[g] * decay[n]).astype(BF16), sol[n])
                      for n, (g, c) in enumerate(items)]
            for n, (g, c) in enumerate(items):
                slot = slot0 + cis[g]
                b_ref[c, slot] = kw[n][:, :DV]
                aq_ref[c, slot, 0:DK, :] = (-kw[n][:, DV:]).astype(BF16)
                if want_out:
                    aq_ref[c, slot, DK:DK + C, :] = (qb[g].astype(F32) * eg[n]
                                                     - aw[n][:, DV:]).astype(BF16)
            if want_out:
                for g in range(group):
                    for e in range(2):
                        oacc_ref[pl.ds(r0s[g], C), e * DV:(e + 1) * DV] += (
                            aw[4 * g + e][:, :DV] + aw[4 * g + 2 + e][:, :DV])
            return carry
        return body

    def phase_b(n, slot0, want_out):
        def body(i, carry):
            states = [s_ref[c] for c in chains]
            cis = [i if c // 2 == 0 else n - 1 - i for c in chains]
            res = []
            for c in chains:
                slot = slot0 + cis[c]
                lhs = aq_ref[c, slot] if want_out else aq_ref[c, slot, 0:DK, :]
                res.append(_dot(lhs, states[c].astype(BF16)))
            for c in chains:
                slot = slot0 + cis[c]
                s_ref[c] = states[c] * gl_ref[c, slot, 0:1, :] + res[c][:DK] + b_ref[c, slot]
                if want_out:
                    r0 = pl.multiple_of(cis[c] * C, C)
                    e = c % 2
                    oacc_ref[pl.ds(r0, C), e * DV:(e + 1) * DV] += res[c][DK:]
            return carry
        return body

    grp_c = math.gcd(n_c, PHASE_A_GROUP)
    grp_l = math.gcd(n_l, PHASE_A_GROUP)
    lax.fori_loop(0, n_c // grp_c,
                  phase_a(qc_ref, kc_ref, vc_ref, grc_ref, colc_ref, 0, False, grp_c), 0)
    lax.fori_loop(0, n_l // grp_l,
                  phase_a(ql_ref, kl_ref, vl_ref, grl_ref, coll_ref, n_c, True, grp_l), 0)
    lax.fori_loop(0, n_c, phase_b(n_c, 0, False), 0)
    lax.fori_loop(0, n_l, phase_b(n_l, n_c, True), 0)

    rows = min(512, seq_l)

    def epilogue(t, carry):
        base = pl.multiple_of(t * rows, 8)
        for e in range(2):
            o = oacc_ref[pl.ds(base, rows), e * DV:(e + 1) * DV]
            ms = jnp.mean(o * o, axis=-1, keepdims=True)
            y = o * lax.rsqrt(ms + EPS) * ng_ref[...]
            zz = z_ref[0, pl.ds(base, rows), e * DV:(e + 1) * DV]
            o_ref[0, pl.ds(base, rows), e * DV:(e + 1) * DV] = (y * _silu(zz)).astype(o_ref.dtype)
        return carry

    lax.fori_loop(0, seq_l // rows, epilogue, 0)


def _delta(qc, kc, vc, grc, ql, kl, vl, grl, p_lat, z_col0, norm_g, chunk):
    b, seq_c, _ = qc.shape
    seq_l = ql.shape[1]
    n_c, n_l = seq_c // chunk, seq_l // chunk
    zoff = z_col0 // (2 * DV)

    def tok(seq, width):
        return pl.BlockSpec((1, seq, width), lambda i, h: (i, 0, h))

    def gate_spec(n):
        return pl.BlockSpec((1, 1, n, 8, chunk), lambda i, h: (i, h, 0, 0, 0))

    return pl.pallas_call(
        functools.partial(_delta_kernel, chunk=chunk, seq_c=seq_c, seq_l=seq_l),
        out_shape=jax.ShapeDtypeStruct((b, seq_l, HV * DV), BF16),
        grid=(b, HK),
        in_specs=[tok(seq_c, DK), tok(seq_c, DK), tok(seq_c, 2 * DV), gate_spec(n_c),
                  tok(seq_l, DK), tok(seq_l, DK), tok(seq_l, 2 * DV), gate_spec(n_l),
                  pl.BlockSpec((1, seq_l, 2 * DV), lambda i, h: (i, 0, h + zoff)),
                  pl.BlockSpec((1, DV), lambda i, h: (0, 0))],
        out_specs=tok(seq_l, 2 * DV),
        scratch_shapes=[pltpu.VMEM((seq_c, LANES), F32),
                        pltpu.VMEM((seq_l, LANES), F32),
                        pltpu.VMEM((4, n_c + n_l, DK + chunk, DV), BF16),
                        pltpu.VMEM((4, n_c + n_l, DK, DV), F32),
                        pltpu.VMEM((4, n_c + n_l, 8, LANES), F32),
                        pltpu.VMEM((4, DK, DV), F32),
                        pltpu.VMEM((seq_l, 2 * DV), F32)],
        compiler_params=_cparams(("parallel", "parallel"), 56),
        name="dn_delta",
    )(qc, kc, vc, grc, ql, kl, vl, grl, p_lat, norm_g.reshape(1, DV))


def _chan_dft_kernel(tab_ref, w_ref, o_ref):
    o_ref[...] = _dot(tab_ref[0], w_ref[...]).astype(o_ref.dtype)


def _chan_dft_weights(tabs, w):
    d = w.shape[0]
    g = tabs.shape[1]
    tn = 512
    nj = d // tn
    return pl.pallas_call(
        _chan_dft_kernel,
        out_shape=jax.ShapeDtypeStruct((d, 2 * d), BF16),
        grid=(d // g, 2, nj),
        in_specs=[pl.BlockSpec((1, g, g), lambda i, t, j: (t, 0, 0)),
                  pl.BlockSpec((g, tn), lambda i, t, j: (i, j))],
        out_specs=pl.BlockSpec((g, tn), lambda i, t, j: (i, t * nj + j)),
        compiler_params=_cparams(("parallel", "parallel", "parallel"), 32),
        name="fourier_chan",
    )(tabs, w)


def _seq_dft_kernel(cos_ref, sin_ref, a_ref, b_ref, mod_ref, r_ref, o_ref, *, scale):
    acc = _dot(cos_ref[...], a_ref[0]) - _dot(sin_ref[...], b_ref[0])
    o_ref[0] = r_ref[0] + (mod_ref[0, 2:3, :] * scale) * acc


def _seq_dft(cos_t, sin_t, ab, modv, resid, scale):
    b, seq, d2 = ab.shape
    d = d2 // 2
    tm, tn = 512, 512
    nj = d // tn
    return pl.pallas_call(
        functools.partial(_seq_dft_kernel, scale=scale),
        out_shape=jax.ShapeDtypeStruct((b, seq, d), F32),
        grid=(b, seq // tm, nj),
        in_specs=[pl.BlockSpec((tm, seq), lambda n, i, j: (i, 0)),
                  pl.BlockSpec((tm, seq), lambda n, i, j: (i, 0)),
                  pl.BlockSpec((1, seq, tn), lambda n, i, j: (n, 0, j)),
                  pl.BlockSpec((1, seq, tn), lambda n, i, j: (n, 0, j + nj)),
                  pl.BlockSpec((1, 8, tn), lambda n, i, j: (n, 0, j)),
                  pl.BlockSpec((1, tm, tn), lambda n, i, j: (n, i, j))],
        out_specs=pl.BlockSpec((1, tm, tn), lambda n, i, j: (n, i, j)),
        compiler_params=_cparams(("parallel", "parallel", "arbitrary"), 48),
        name="fourier_seq",
    )(cos_t, sin_t, ab, ab, modv, resid)


def _dft_table_kernel(cos_ref, sin_ref, tc_ref, ts_ref, *, n, tk):
    w = 2.0 * math.pi / n

    def unit(phase):
        ang = (phase & (n - 1)).astype(F32) * w
        return jnp.cos(ang), jnp.sin(ang)

    @pl.when(pl.program_id(0) == 0)
    def _():
        a = lax.broadcasted_iota(jnp.int32, (tk, n), 0)
        m = lax.broadcasted_iota(jnp.int32, (tk, n), 1)
        tc_ref[...], ts_ref[...] = unit(a * m)

    m1 = lax.broadcasted_iota(jnp.int32, (1, n), 1)
    rc, rs = unit((pl.program_id(0) * tk) * m1)
    tcv, tsv = tc_ref[...], ts_ref[...]
    cos_ref[...] = (rc * tcv - rs * tsv).astype(cos_ref.dtype)
    sin_ref[...] = (rs * tcv + rc * tsv).astype(sin_ref.dtype)


def _dft_tables(n):
    tk = min(256, n)
    out = jax.ShapeDtypeStruct((n, n), BF16)
    spec = pl.BlockSpec((tk, n), lambda i: (i, 0))
    return pl.pallas_call(
        functools.partial(_dft_table_kernel, n=n, tk=tk),
        out_shape=(out, out),
        grid=(n // tk,),
        in_specs=[],
        out_specs=(spec, spec),
        scratch_shapes=[pltpu.VMEM((tk, n), F32), pltpu.VMEM((tk, n), F32)],
        compiler_params=_cparams(("arbitrary",), 40),
        name="dft_tables",
    )()


def _gate_rows(p_ba, chunk):
    b, s, _ = p_ba.shape
    t = p_ba.reshape(b, s, 2, 2, HK, 2)
    return jnp.transpose(t, (0, 4, 2, 3, 5, 1)).reshape(b, 4 * HV, s)


def _gate_params(a_log, dt_bias):
    def rows(v, kind_fill):
        t = jnp.stack([jnp.full_like(v, kind_fill), v]).reshape(2, 2, HK, 2)
        return jnp.transpose(t, (2, 0, 1, 3)).reshape(4 * HV, 1)
    kind = rows(jnp.ones((2, HV), F32), 0.0)
    dirn = rows(jnp.stack([jnp.zeros((HV,), F32), jnp.ones((HV,), F32)]), 0.0)
    return rows(a_log.astype(F32), 0.0), rows(dt_bias.astype(F32), 0.0), kind, dirn


def _chunk_major(rows_t, chunk):
    b, _, s = rows_t.shape
    t = rows_t.reshape(b, HK, 8, s // chunk, chunk)
    return jnp.transpose(t, (0, 1, 3, 2, 4))


def kernel(x, c, ctx, c_ctx, norm_g, mod_w, mod_b, ffn_w_gate, ffn_w_up, ffn_w_down, dn_w_in,
           dn_conv_w, dn_a_log, dn_dt_bias, dn_norm_g, dn_w_out, fn_w_out, final_norm_g):
    bsz, seq, d = x.shape
    seq_c = ctx.shape[1]
    t_lat, t_ctx = bsz * seq, bsz * seq_c
    key_dim, val_dim = HK * DK, HV * DV
    qkv_dim = 2 * key_dim + val_dim
    chunk = DELTA_CHUNK

    cond = jnp.zeros((8, d), F32).at[:bsz].set(c).at[bsz].set(c_ctx)
    mod = _modulation(cond, mod_w, mod_b)

    def modv(layer, sub, ctx_rows=False):
        m3 = mod[layer, :, 3 * sub * d:3 * (sub + 1) * d].reshape(8, 3, d)
        m3 = m3[bsz:bsz + 1] if ctx_rows else m3[:bsz]
        return jnp.pad(m3, ((0, 0), (0, 5), (0, 0)))

    w_ffn = (ffn_w_gate.astype(BF16), ffn_w_up.astype(BF16), ffn_w_down.astype(BF16))

    h = x.reshape(t_lat, d)
    hc = ctx.reshape(t_ctx, d)

    h = _ffn(h, modv(0, 0), norm_g[0, 0], *w_ffn, 0, 0)
    hc = _ffn(hc, modv(0, 0, True), norm_g[0, 0], *w_ffn, 0, 0)

    n_in = dn_w_in.shape[2]
    w_in = dn_w_in[0].astype(BF16)
    p = _matmul(h, w_in, out_dtype=F32, tm=1024, tn=512, modv=modv(0, 1), norm_g=norm_g[0, 1],
                name="dn_in_proj").reshape(bsz, seq, n_in)
    pc = _matmul(hc, w_in, out_dtype=F32, tm=512, tn=512, modv=modv(0, 1, True),
                 norm_g=norm_g[0, 1], name="dn_in_proj_ctx").reshape(bsz, seq_c, n_in)

    conv_w = dn_conv_w[0]
    a_col, dt_col, kind_col, dir_col = _gate_params(dn_a_log[0], dn_dt_bias[0])

    def dn_inputs(pp):
        q = _conv_act(pp, conv_w, 0, key_dim, l2norm=True, out_scale=DK ** -0.5)
        k = _conv_act(pp, conv_w, key_dim, key_dim, l2norm=True)
        v = _conv_act(pp, conv_w, 2 * key_dim, val_dim, l2norm=False)
        ba = pp[:, :, qkv_dim + val_dim:qkv_dim + val_dim + 4 * HV]
        gr = _gates(_gate_rows(ba, chunk), a_col, dt_col, kind_col, dir_col, chunk)
        return q, k, v, _chunk_major(gr, chunk)

    ql, kl, vl, grl = dn_inputs(p)
    qc, kc, vc, grc = dn_inputs(pc)
    y = _delta(qc, kc, vc, grc, ql, kl, vl, grl, p, qkv_dim, dn_norm_g[0], chunk)
    h = _matmul(y.reshape(t_lat, val_dim), dn_w_out[0].astype(BF16), out_dtype=F32, tm=1024,
                tn=512, modv=modv(0, 1), resid=h, name="dn_out_proj")
    h = _ffn(h, modv(0, 2), norm_g[0, 2], *w_ffn, 0, 1)

    h = _ffn(h, modv(1, 0), norm_g[1, 0], *w_ffn, 1, 0)
    gdim = d // N_FGROUPS
    cos_c, sin_c = _dft_tables(gdim)
    w_cs = _chan_dft_weights(jnp.stack([cos_c, sin_c]), fn_w_out[0].astype(BF16))
    ab = _matmul(h, w_cs, out_dtype=BF16, tm=1024, tn=512, modv=modv(1, 1), norm_g=norm_g[1, 1],
                 name="fourier_in").reshape(bsz, seq, 2 * d)
    cos_l, sin_l = _dft_tables(seq)
    h = _seq_dft(cos_l, sin_l, ab, modv(1, 1), h.reshape(bsz, seq, d),
                 1.0 / math.sqrt(seq * gdim)).reshape(t_lat, d)
    h = _ffn(h, modv(1, 2), norm_g[1, 2], *w_ffn, 1, 1, final_g=final_norm_g)
    return h.reshape(bsz, seq, d)
```

```python
import functools
import math

import jax
import jax.numpy as jnp
from jax import lax
from jax.experimental import pallas as pl
from jax.experimental.pallas import tpu as pltpu

F32 = jnp.float32
BF16 = jnp.bfloat16

EPS = 1e-6
FFN_RES = 0.5
N_SUB = 3
HK = 16
HV = 32
DK = 128
DV = 128
CONV_W = 5
N_FGROUPS = 4
DELTA_CHUNK = 128
PHASE_A_GROUP = 4

LANES = 128
MIB = 1024 * 1024


def _cparams(semantics, vmem_mib):
    return pltpu.CompilerParams(dimension_semantics=semantics,
                                vmem_limit_bytes=int(vmem_mib * MIB))


def _dot(a, b):
    return jnp.dot(a, b, preferred_element_type=F32)


def _dot_nt(a, b):
    return lax.dot_general(a, b, (((1,), (1,)), ((), ())), preferred_element_type=F32)


def _dot_tn(a, b):
    return lax.dot_general(a, b, (((0,), (0,)), ((), ())), preferred_element_type=F32)


def _silu(x):
    return x * jax.nn.sigmoid(x)


_NORM_ROWS = 32


def _ada_norm_into(h_ref, x_ref, g_ref, mod_ref):
    gain = g_ref[...] * (1.0 + mod_ref[0, 1:2, :])
    shift = mod_ref[0, 0:1, :]
    rows = min(_NORM_ROWS, x_ref.shape[0])

    def step(t, carry):
        r0 = pl.multiple_of(t * rows, rows)
        x = x_ref[pl.ds(r0, rows), :]
        ms = jnp.mean(x * x, axis=-1, keepdims=True)
        h_ref[pl.ds(r0, rows), :] = (x * lax.rsqrt(ms + EPS) * gain + shift).astype(BF16)
        return carry

    lax.fori_loop(0, x_ref.shape[0] // rows, step, 0, unroll=2)


def _mod_kernel(cond_ref, w_ref, b_ref, o_ref):
    s = _silu(cond_ref[...]).astype(BF16)
    o_ref[0] = _dot(s, w_ref[0].astype(BF16)) + b_ref[0]


def _modulation(cond, mod_w, mod_b):
    depth, d, n = mod_w.shape
    tn = 1024
    return pl.pallas_call(
        _mod_kernel,
        out_shape=jax.ShapeDtypeStruct((depth, 8, n), F32),
        grid=(depth, n // tn),
        in_specs=[pl.BlockSpec((8, d), lambda l, j: (0, 0)),
                  pl.BlockSpec((1, d, tn), lambda l, j: (l, 0, j)),
                  pl.BlockSpec((1, 1, tn), lambda l, j: (l, 0, j))],
        out_specs=pl.BlockSpec((1, 8, tn), lambda l, j: (l, 0, j)),
        compiler_params=_cparams(("parallel", "parallel"), 40),
        name="modulation",
    )(cond, mod_w, mod_b.reshape(depth, 1, n))


def _ffn_kernel(x_ref, mod_ref, g_ref, wg_ref, wu_ref, wd_ref, fg_ref, o_ref, h_ref, *,
                final_norm):
    j = pl.program_id(1)
    nj = pl.num_programs(1)

    @pl.when(j == 0)
    def _():
        _ada_norm_into(h_ref, x_ref, g_ref, mod_ref)
        o_ref[...] = jnp.zeros_like(o_ref)

    h = h_ref[...]
    a = (_silu(_dot(h, wg_ref[...].astype(BF16)))
         * _dot(h, wu_ref[...].astype(BF16))).astype(BF16)
    o_ref[...] += _dot(a, wd_ref[...].astype(BF16))

    @pl.when(j == nj - 1)
    def _():
        gate = FFN_RES * mod_ref[0, 2:3, :]
        rows = min(_NORM_ROWS, x_ref.shape[0])

        def step(t, carry):
            r0 = pl.multiple_of(t * rows, rows)
            r = x_ref[pl.ds(r0, rows), :] + gate * o_ref[pl.ds(r0, rows), :]
            if final_norm:
                ms = jnp.mean(r * r, axis=-1, keepdims=True)
                r = r * lax.rsqrt(ms + EPS) * fg_ref[...]
            o_ref[pl.ds(r0, rows), :] = r
            return carry

        lax.fori_loop(0, x_ref.shape[0] // rows, step, 0, unroll=2)


def _ffn(x, modv, g, wg, wu, wd, layer, which, final_g=None):
    t, d = x.shape
    f = wg.shape[3]
    tm = min(1024, t // modv.shape[0])
    tf = 256
    tiles_per_mod = t // tm // modv.shape[0]
    fg = (final_g if final_g is not None else g).reshape(1, d)
    return pl.pallas_call(
        functools.partial(_ffn_kernel, final_norm=final_g is not None),
        out_shape=jax.ShapeDtypeStruct((t, d), F32),
        grid=(t // tm, f // tf),
        in_specs=[pl.BlockSpec((tm, d), lambda i, j: (i, 0)),
                  pl.BlockSpec((1, 8, d), lambda i, j: (i // tiles_per_mod, 0, 0)),
                  pl.BlockSpec((1, d), lambda i, j: (0, 0)),
                  pl.BlockSpec((None, None, d, tf), lambda i, j: (layer, which, 0, j)),
                  pl.BlockSpec((None, None, d, tf), lambda i, j: (layer, which, 0, j)),
                  pl.BlockSpec((None, None, tf, d), lambda i, j: (layer, which, j, 0)),
                  pl.BlockSpec((1, d), lambda i, j: (0, 0))],
        out_specs=pl.BlockSpec((tm, d), lambda i, j: (i, 0)),
        scratch_shapes=[pltpu.VMEM((tm, d), BF16)],
        compiler_params=_cparams(("parallel", "arbitrary"), 56),
        name="ffn_half",
    )(x, modv, g.reshape(1, d), wg, wu, wd, fg)


def _mm_kernel(*refs, norm, resid):
    it = iter(refs)
    a_ref = next(it)
    b_ref = next(it)
    if norm:
        mod_ref, g_ref = next(it), next(it)
    elif resid:
        mod_ref = next(it)
    if resid:
        r_ref = next(it)
    o_ref = next(it)

    if norm:
        h_ref = next(it)

        @pl.when(pl.program_id(1) == 0)
        def _():
            _ada_norm_into(h_ref, a_ref, g_ref, mod_ref)
        a = h_ref[...]
    else:
        a = a_ref[...]
    acc = _dot(a, b_ref[...])
    if resid:
        acc = r_ref[...] + mod_ref[0, 2:3, :] * acc
    o_ref[...] = acc.astype(o_ref.dtype)


def _matmul(a, b, *, out_dtype, tm, tn, modv=None, norm_g=None, resid=None, vmem_mib=48,
            name="matmul"):
    m, k = a.shape
    n = b.shape[1]
    tm = min(tm, m if modv is None else m // modv.shape[0])
    norm = norm_g is not None
    has_res = resid is not None
    in_specs = [pl.BlockSpec((tm, k), lambda i, j: (i, 0)),
                pl.BlockSpec((k, tn), lambda i, j: (0, j))]
    args = [a, b]
    if norm or has_res:
        tiles_per_mod = m // tm // modv.shape[0]
        if norm:
            in_specs.append(pl.BlockSpec((1, 8, k), lambda i, j: (i // tiles_per_mod, 0, 0)))
            in_specs.append(pl.BlockSpec((1, k), lambda i, j: (0, 0)))
            args += [modv, norm_g.reshape(1, k)]
        else:
            in_specs.append(pl.BlockSpec((1, 8, tn), lambda i, j: (i // tiles_per_mod, 0, j)))
            args.append(modv)
    if has_res:
        in_specs.append(pl.BlockSpec((tm, tn), lambda i, j: (i, j)))
        args.append(resid)
    return pl.pallas_call(
        functools.partial(_mm_kernel, norm=norm, resid=has_res),
        out_shape=jax.ShapeDtypeStruct((m, n), out_dtype),
        grid=(m // tm, pl.cdiv(n, tn)),
        in_specs=in_specs,
        out_specs=pl.BlockSpec((tm, tn), lambda i, j: (i, j)),
        scratch_shapes=[pltpu.VMEM((tm, k), BF16)] if norm else [],
        compiler_params=_cparams(("parallel", "arbitrary"), vmem_mib),
        name=name,
    )(*args)


_CONV_ROWS = 256
_CONV_HALO = 8


def _conv_kernel(x_ref, w_ref, o_ref, xp_ref, *, seq, tc, l2norm, out_scale):
    zeros = jnp.zeros((_CONV_HALO, tc), F32)
    xp_ref[0:_CONV_HALO, :] = zeros
    xp_ref[_CONV_HALO + seq:, :] = zeros
    xp_ref[_CONV_HALO:_CONV_HALO + seq, :] = x_ref[0]
    w = w_ref[...]
    rows = min(_CONV_ROWS, seq)

    def step(t):
        base = t * rows
        acc = None
        for j in range(CONV_W):
            start = base + _CONV_HALO - CONV_W // 2 + j
            term = xp_ref[start:start + rows, :] * w[j:j + 1, :]
            acc = term if acc is None else acc + term
        y = _silu(acc)
        if l2norm:
            parts = []
            for hh in range(tc // LANES):
                yh = y[:, hh * LANES:(hh + 1) * LANES]
                ss = jnp.sum(yh * yh, axis=-1, keepdims=True)
                parts.append(yh * (lax.rsqrt(ss + EPS) * out_scale))
            y = parts[0] if len(parts) == 1 else jnp.concatenate(parts, axis=-1)
        o_ref[0, base:base + rows, :] = y.astype(o_ref.dtype)

    for t in range(seq // rows):
        step(t)


def _conv_act(p, conv_w, col0, ncols, *, l2norm, out_scale=1.0):
    b, s, _ = p.shape
    tc = 256
    off = col0 // tc
    return pl.pallas_call(
        functools.partial(_conv_kernel, seq=s, tc=tc, l2norm=l2norm, out_scale=out_scale),
        out_shape=jax.ShapeDtypeStruct((b, s, ncols), BF16),
        grid=(b, ncols // tc),
        in_specs=[pl.BlockSpec((1, s, tc), lambda i, j: (i, 0, j + off)),
                  pl.BlockSpec((CONV_W, tc), lambda i, j: (0, j + off))],
        out_specs=pl.BlockSpec((1, s, tc), lambda i, j: (i, 0, j)),
        scratch_shapes=[pltpu.VMEM((s + 2 * _CONV_HALO, tc), F32)],
        compiler_params=_cparams(("parallel", "parallel"), 40),
        name="dn_conv",
    )(p, conv_w)


def _split3(x):
    x1 = x.astype(BF16)
    r1 = x - x1.astype(F32)
    x2 = r1.astype(BF16)
    x3 = (r1 - x2.astype(F32)).astype(BF16)
    return x1, x2, x3


def _gates_kernel(ba_ref, a_ref, dt_ref, kind_ref, dir_ref, o_ref, *, chunk, tl):
    x = ba_ref[0]
    beta = jax.nn.sigmoid(x)
    y = x + dt_ref[...]
    softplus = jnp.maximum(y, 0.0) + jnp.log(1.0 + jnp.exp(-jnp.abs(y)))
    g = -jnp.exp(a_ref[...]) * softplus
    i = lax.broadcasted_iota(jnp.int32, (tl, tl), 0)
    j = lax.broadcasted_iota(jnp.int32, (tl, tl), 1)
    sh = int(math.log2(chunk))
    same = lax.shift_right_logical(i, sh) == lax.shift_right_logical(j, sh)
    pre = jnp.where(same & (i <= j), 1.0, 0.0).astype(BF16)
    suf = jnp.where(same & (i >= j), 1.0, 0.0).astype(BF16)
    g1, g2, g3 = _split3(g)
    cpre = _dot(g1, pre) + _dot(g2, pre) + _dot(g3, pre)
    csuf = _dot(g1, suf) + _dot(g2, suf) + _dot(g3, suf)
    cum = jnp.where(dir_ref[...] > 0.5, csuf, cpre)
    o_ref[0] = jnp.where(kind_ref[...] > 0.5, cum, beta)


def _gates(ba_t, a_col, dt_col, kind_col, dir_col, chunk):
    b, r, s = ba_t.shape
    tl = min(512, s)
    col = pl.BlockSpec((r, 1), lambda i, j: (0, 0))
    return pl.pallas_call(
        functools.partial(_gates_kernel, chunk=chunk, tl=tl),
        out_shape=jax.ShapeDtypeStruct((b, r, s), F32),
        grid=(b, s // tl),
        in_specs=[pl.BlockSpec((1, r, tl), lambda i, j: (i, 0, j)), col, col, col, col],
        out_specs=pl.BlockSpec((1, r, tl), lambda i, j: (i, 0, j)),
        compiler_params=_cparams(("parallel", "parallel"), 32),
        name="dn_gates",
    )(ba_t, a_col, dt_col, kind_col, dir_col)


def _col_form(gr_ref, dst_ref, n_rows, chunk):
    pad = jnp.zeros((LANES - 8, LANES), F32)
    for s in range(n_rows // LANES):
        if chunk <= LANES:
            per = LANES // chunk
            parts = [gr_ref[0, 0, s * per + t] for t in range(per)]
            slab = parts[0] if per == 1 else jnp.concatenate(parts, axis=-1)
        else:
            per = chunk // LANES
            slab = gr_ref[0, 0, s // per, :, (s % per) * LANES:(s % per + 1) * LANES]
        dst_ref[s * LANES:(s + 1) * LANES, :] = jnp.concatenate([slab, pad], axis=0).T


def _delta_kernel(qc_ref, kc_ref, vc_ref, grc_ref, ql_ref, kl_ref, vl_ref, grl_ref,
                  z_ref, ng_ref, o_ref, colc_ref, coll_ref, aq_ref, b_ref, gl_ref, s_ref,
                  oacc_ref, *, chunk, seq_c, seq_l):
    C = chunk
    n_c, n_l = seq_c // C, seq_l // C
    _col_form(grc_ref, colc_ref, seq_c, C)
    _col_form(grl_ref, coll_ref, seq_l, C)
    s_ref[...] = jnp.zeros_like(s_ref)
    oacc_ref[...] = jnp.zeros_like(oacc_ref)

    row = lax.broadcasted_iota(jnp.int32, (C, C), 0)
    col = lax.broadcasted_iota(jnp.int32, (C, C), 1)
    eye = jnp.where(row == col, 1.0, 0.0)

    def same_block(n):
        return lax.shift_right_logical(row, n) == lax.shift_right_logical(col, n)

    pair_mask = jnp.where(same_block(1), 1.0, 0.0)
    join_masks = [jnp.where(same_block(n + 1), 1.0, 0.0) - jnp.where(same_block(n), 1.0, 0.0)
                  for n in range(1, int(math.log2(C)))]

    chains = range(4)

    def take_rows(a, blk, d):
        off = blk if d == 0 else 0
        return jnp.concatenate([a[s + off:s + off + blk] for s in range(0, C, 2 * blk)], axis=0)

    def put_rows(a, new, blk, d):
        parts = []
        for i, s in enumerate(range(0, C, 2 * blk)):
            upd = new[i * blk:(i + 1) * blk]
            parts += [a[s:s + blk], upd] if d == 0 else [upd, a[s + blk:s + 2 * blk]]
        return jnp.concatenate(parts, axis=0)

    def phase_a(q_ref, k_ref, v_ref, gr_ref, col_ref, slot0, want_out, group):
        items = [(g, c) for g in range(group) for c in chains]

        def body(it, carry):
            cis = [it * group + g for g in range(group)]
            r0s = [pl.multiple_of(ci * C, C) for ci in cis]
            kb = [k_ref[0, pl.ds(r0, C), :] for r0 in r0s]
            kf = [t.astype(F32) for t in kb]
            gram = [_dot_nt(t, t) for t in kb]
            if want_out:
                qb = [q_ref[0, pl.ds(r0, C), :] for r0 in r0s]
                qk = [_dot_nt(qb[g], kb[g]) for g in range(group)]
            cols = [col_ref[pl.ds(r0, C), :] for r0 in r0s]
            beta, gcol, decay, lm, x = [], [], [], [], []
            for n, (g, c) in enumerate(items):
                d = c // 2
                beta.append(cols[g][:, c:c + 1])
                gcol.append(cols[g][:, 4 + c:5 + c])
                grow = gr_ref[0, 0, cis[g], 4 + c:5 + c, :]
                incl = (row >= col) if d == 0 else (row <= col)
                strict = (row > col) if d == 0 else (row < col)
                decay.append(jnp.exp(jnp.where(incl, gcol[n] - grow, -jnp.inf)))
                lm.append(jnp.where(strict, gram[g] * decay[n] * beta[n], 0.0))
                x.append(eye - lm[n] * pair_mask)
            idx = range(len(items))
            for lvl, jm in enumerate(join_masks, start=1):
                blk = 1 << lvl
                if blk < 8:
                    y = [_dot((lm[n] * jm).astype(BF16), x[n].astype(BF16)) for n in idx]
                    x = [x[n] - _dot(x[n].astype(BF16), y[n].astype(BF16)) for n in idx]
                    continue
                jmh = [take_rows(jm, blk, d) for d in range(2)]
                zeros = jnp.zeros((blk, C), F32)
                y = [_dot((take_rows(lm[n], blk, items[n][1] // 2)
                           * jmh[items[n][1] // 2]).astype(BF16), x[n].astype(BF16)) for n in idx]
                new = []
                for n in idx:
                    d = items[n][1] // 2
                    yfull = jnp.concatenate(
                        [t for s in range(C // (2 * blk))
                         for t in ((zeros, y[n][s * blk:(s + 1) * blk]) if d == 0
                                   else (y[n][s * blk:(s + 1) * blk], zeros))], axis=0)
                    xa = take_rows(x[n], blk, d)
                    new.append(xa - _dot(xa.astype(BF16), yfull.astype(BF16)))
                x = [put_rows(x[n], new[n], blk, items[n][1] // 2) for n in idx]
            eg = [jnp.exp(gcol[n]) for n in idx]
            sol = []
            for n, (g, c) in enumerate(items):
                e = c % 2
                vf = v_ref[0, pl.ds(r0s[g], C), e * DV:(e + 1) * DV].astype(F32)
                rhs = jnp.concatenate([vf * beta[n], kf[g] * (beta[n] * eg[n])], axis=-1)
                sol.append(_dot(x[n].astype(BF16), rhs.astype(BF16)).astype(BF16))
            kw = []
            for n, (g, c) in enumerate(items):
                glast = gcol[n][C - 1:C, :] if c // 2 == 0 else gcol[n][0:1, :]
                ktail = (kf[g] * jnp.exp(glast - gcol[n])).astype(BF16)
                kw.append(_dot_tn(ktail, sol[n]))
                gl_ref[c, slot0 + cis[g]] = jnp.broadcast_to(jnp.exp(glast), (8, LANES))
            if want_out:
                aw = [_dot((qk[g] * decay[n]).astype(BF16), sol[n])
                      for n, (g, c) in enumerate(items)]
            for n, (g, c) in enumerate(items):
                slot = slot0 + cis[g]
                b_ref[c, slot] = kw[n][:, :DV]
                aq_ref[c, slot, 0:DK, :] = (-kw[n][:, DV:]).astype(BF16)
                if want_out:
                    aq_ref[c, slot, DK:DK + C, :] = (qb[g].astype(F32) * eg[n]
                                                     - aw[n][:, DV:]).astype(BF16)
            if want_out:
                for g in range(group):
                    for e in range(2):
                        oacc_ref[pl.ds(r0s[g], C), e * DV:(e + 1) * DV] += (
                            aw[4 * g + e][:, :DV] + aw[4 * g + 2 + e][:, :DV])
            return carry
        return body

    def phase_b(n, slot0, want_out):
        def body(i, carry):
            states = [s_ref[c] for c in chains]
            cis = [i if c // 2 == 0 else n - 1 - i for c in chains]
            res = []
            for c in chains:
                slot = slot0 + cis[c]
                lhs = aq_ref[c, slot] if want_out else aq_ref[c, slot, 0:DK, :]
                res.append(_dot(lhs, states[c].astype(BF16)))
            for c in chains:
                slot = slot0 + cis[c]
                s_ref[c] = states[c] * gl_ref[c, slot, 0:1, :] + res[c][:DK] + b_ref[c, slot]
                if want_out:
                    r0 = pl.multiple_of(cis[c] * C, C)
                    e = c % 2
                    oacc_ref[pl.ds(r0, C), e * DV:(e + 1) * DV] += res[c][DK:]
            return carry
        return body

    grp_c = math.gcd(n_c, PHASE_A_GROUP)
    grp_l = math.gcd(n_l, PHASE_A_GROUP)
    lax.fori_loop(0, n_c // grp_c,
                  phase_a(qc_ref, kc_ref, vc_ref, grc_ref, colc_ref, 0, False, grp_c), 0)
    lax.fori_loop(0, n_l // grp_l,
                  phase_a(ql_ref, kl_ref, vl_ref, grl_ref, coll_ref, n_c, True, grp_l), 0)
    lax.fori_loop(0, n_c, phase_b(n_c, 0, False), 0)
    lax.fori_loop(0, n_l, phase_b(n_l, n_c, True), 0)

    rows = min(512, seq_l)

    def epilogue(t, carry):
        base = pl.multiple_of(t * rows, 8)
        for e in range(2):
            o = oacc_ref[pl.ds(base, rows), e * DV:(e + 1) * DV]
            ms = jnp.mean(o * o, axis=-1, keepdims=True)
            y = o * lax.rsqrt(ms + EPS) * ng_ref[...]
            zz = z_ref[0, pl.ds(base, rows), e * DV:(e + 1) * DV]
            o_ref[0, pl.ds(base, rows), e * DV:(e + 1) * DV] = (y * _silu(zz)).astype(o_ref.dtype)
        return carry

    lax.fori_loop(0, seq_l // rows, epilogue, 0)


def _delta(qc, kc, vc, grc, ql, kl, vl, grl, p_lat, z_col0, norm_g, chunk):
    b, seq_c, _ = qc.shape
    seq_l = ql.shape[1]
    n_c, n_l = seq_c // chunk, seq_l // chunk
    zoff = z_col0 // (2 * DV)

    def tok(seq, width):
        return pl.BlockSpec((1, seq, width), lambda i, h: (i, 0, h))

    def gate_spec(n):
        return pl.BlockSpec((1, 1, n, 8, chunk), lambda i, h: (i, h, 0, 0, 0))

    return pl.pallas_call(
        functools.partial(_delta_kernel, chunk=chunk, seq_c=seq_c, seq_l=seq_l),
        out_shape=jax.ShapeDtypeStruct((b, seq_l, HV * DV), BF16),
        grid=(b, HK),
        in_specs=[tok(seq_c, DK), tok(seq_c, DK), tok(seq_c, 2 * DV), gate_spec(n_c),
                  tok(seq_l, DK), tok(seq_l, DK), tok(seq_l, 2 * DV), gate_spec(n_l),
                  pl.BlockSpec((1, seq_l, 2 * DV), lambda i, h: (i, 0, h + zoff)),
                  pl.BlockSpec((1, DV), lambda i, h: (0, 0))],
        out_specs=tok(seq_l, 2 * DV),
        scratch_shapes=[pltpu.VMEM((seq_c, LANES), F32),
                        pltpu.VMEM((seq_l, LANES), F32),
                        pltpu.VMEM((4, n_c + n_l, DK + chunk, DV), BF16),
                        pltpu.VMEM((4, n_c + n_l, DK, DV), F32),
                        pltpu.VMEM((4, n_c + n_l, 8, LANES), F32),
                        pltpu.VMEM((4, DK, DV), F32),
                        pltpu.VMEM((seq_l, 2 * DV), F32)],
        compiler_params=_cparams(("parallel", "parallel"), 56),
        name="dn_delta",
    )(qc, kc, vc, grc, ql, kl, vl, grl, p_lat, norm_g.reshape(1, DV))


def _chan_dft_kernel(tab_ref, w_ref, o_ref):
    o_ref[...] = _dot(tab_ref[0], w_ref[...]).astype(o_ref.dtype)


def _chan_dft_weights(tabs, w):
    d = w.shape[0]
    g = tabs.shape[1]
    tn = 512
    nj = d // tn
    return pl.pallas_call(
        _chan_dft_kernel,
        out_shape=jax.ShapeDtypeStruct((d, 2 * d), BF16),
        grid=(d // g, 2, nj),
        in_specs=[pl.BlockSpec((1, g, g), lambda i, t, j: (t, 0, 0)),
                  pl.BlockSpec((g, tn), lambda i, t, j: (i, j))],
        out_specs=pl.BlockSpec((g, tn), lambda i, t, j: (i, t * nj + j)),
        compiler_params=_cparams(("parallel", "parallel", "parallel"), 32),
        name="fourier_chan",
    )(tabs, w)


def _seq_dft_kernel(cos_ref, sin_ref, a_ref, b_ref, mod_ref, r_ref, o_ref, *, scale):
    acc = _dot(cos_ref[...], a_ref[0]) - _dot(sin_ref[...], b_ref[0])
    o_ref[0] = r_ref[0] + (mod_ref[0, 2:3, :] * scale) * acc


def _seq_dft(cos_t, sin_t, ab, modv, resid, scale):
    b, seq, d2 = ab.shape
    d = d2 // 2
    tm, tn = 512, 512
    nj = d // tn
    return pl.pallas_call(
        functools.partial(_seq_dft_kernel, scale=scale),
        out_shape=jax.ShapeDtypeStruct((b, seq, d), F32),
        grid=(b, seq // tm, nj),
        in_specs=[pl.BlockSpec((tm, seq), lambda n, i, j: (i, 0)),
                  pl.BlockSpec((tm, seq), lambda n, i, j: (i, 0)),
                  pl.BlockSpec((1, seq, tn), lambda n, i, j: (n, 0, j)),
                  pl.BlockSpec((1, seq, tn), lambda n, i, j: (n, 0, j + nj)),
                  pl.BlockSpec((1, 8, tn), lambda n, i, j: (n, 0, j)),
                  pl.BlockSpec((1, tm, tn), lambda n, i, j: (n, i, j))],
        out_specs=pl.BlockSpec((1, tm, tn), lambda n, i, j: (n, i, j)),
        compiler_params=_cparams(("parallel", "parallel", "arbitrary"), 48),
        name="fourier_seq",
    )(cos_t, sin_t, ab, ab, modv, resid)


def _dft_table_kernel(cos_ref, sin_ref, tc_ref, ts_ref, *, n, tk):
    w = 2.0 * math.pi / n

    def unit(phase):
        ang = (phase & (n - 1)).astype(F32) * w
        return jnp.cos(ang), jnp.sin(ang)

    @pl.when(pl.program_id(0) == 0)
    def _():
        a = lax.broadcasted_iota(jnp.int32, (tk, n), 0)
        m = lax.broadcasted_iota(jnp.int32, (tk, n), 1)
        tc_ref[...], ts_ref[...] = unit(a * m)

    m1 = lax.broadcasted_iota(jnp.int32, (1, n), 1)
    rc, rs = unit((pl.program_id(0) * tk) * m1)
    tcv, tsv = tc_ref[...], ts_ref[...]
    cos_ref[...] = (rc * tcv - rs * tsv).astype(cos_ref.dtype)
    sin_ref[...] = (rs * tcv + rc * tsv).astype(sin_ref.dtype)


def _dft_tables(n):
    tk = min(256, n)
    out = jax.ShapeDtypeStruct((n, n), BF16)
    spec = pl.BlockSpec((tk, n), lambda i: (i, 0))
    return pl.pallas_call(
        functools.partial(_dft_table_kernel, n=n, tk=tk),
        out_shape=(out, out),
        grid=(n // tk,),
        in_specs=[],
        out_specs=(spec, spec),
        scratch_shapes=[pltpu.VMEM((tk, n), F32), pltpu.VMEM((tk, n), F32)],
        compiler_params=_cparams(("arbitrary",), 40),
        name="dft_tables",
    )()


def _gate_rows(p_ba, chunk):
    b, s, _ = p_ba.shape
    t = p_ba.reshape(b, s, 2, 2, HK, 2)
    return jnp.transpose(t, (0, 4, 2, 3, 5, 1)).reshape(b, 4 * HV, s)


def _gate_params(a_log, dt_bias):
    def rows(v, kind_fill):
        t = jnp.stack([jnp.full_like(v, kind_fill), v]).reshape(2, 2, HK, 2)
        return jnp.transpose(t, (2, 0, 1, 3)).reshape(4 * HV, 1)
    kind = rows(jnp.ones((2, HV), F32), 0.0)
    dirn = rows(jnp.stack([jnp.zeros((HV,), F32), jnp.ones((HV,), F32)]), 0.0)
    return rows(a_log.astype(F32), 0.0), rows(dt_bias.astype(F32), 0.0), kind, dirn


def _chunk_major(rows_t, chunk):
    b, _, s = rows_t.shape
    t = rows_t.reshape(b, HK, 8, s // chunk, chunk)
    return jnp.transpose(t, (0, 1, 3, 2, 4))


def kernel(x, c, ctx, c_ctx, norm_g, mod_w, mod_b, ffn_w_gate, ffn_w_up, ffn_w_down, dn_w_in,
           dn_conv_w, dn_a_log, dn_dt_bias, dn_norm_g, dn_w_out, fn_w_out, final_norm_g):
    bsz, seq, d = x.shape
    seq_c = ctx.shape[1]
    t_lat, t_ctx = bsz * seq, bsz * seq_c
    key_dim, val_dim = HK * DK, HV * DV
    qkv_dim = 2 * key_dim + val_dim
    chunk = DELTA_CHUNK

    cond = jnp.zeros((8, d), F32).at[:bsz].set(c).at[bsz].set(c_ctx)
    mod = _modulation(cond, mod_w, mod_b)

    def modv(layer, sub, ctx_rows=False):
        m3 = mod[layer, :, 3 * sub * d:3 * (sub + 1) * d].reshape(8, 3, d)
        m3 = m3[bsz:bsz + 1] if ctx_rows else m3[:bsz]
        return jnp.pad(m3, ((0, 0), (0, 5), (0, 0)))

    w_ffn = (ffn_w_gate, ffn_w_up, ffn_w_down)

    h = x.reshape(t_lat, d)
    hc = ctx.reshape(t_ctx, d)

    h = _ffn(h, modv(0, 0), norm_g[0, 0], *w_ffn, 0, 0)
    hc = _ffn(hc, modv(0, 0, True), norm_g[0, 0], *w_ffn, 0, 0)

    n_in = dn_w_in.shape[2]
    w_in = dn_w_in[0].astype(BF16)
    p = _matmul(h, w_in, out_dtype=F32, tm=1024, tn=512, modv=modv(0, 1), norm_g=norm_g[0, 1],
                name="dn_in_proj").reshape(bsz, seq, n_in)
    pc = _matmul(hc, w_in, out_dtype=F32, tm=512, tn=512, modv=modv(0, 1, True),
                 norm_g=norm_g[0, 1], name="dn_in_proj_ctx").reshape(bsz, seq_c, n_in)

    conv_w = dn_conv_w[0]
    a_col, dt_col, kind_col, dir_col = _gate_params(dn_a_log[0], dn_dt_bias[0])

    def dn_inputs(pp):
        q = _conv_act(pp, conv_w, 0, key_dim, l2norm=True, out_scale=DK ** -0.5)
        k = _conv_act(pp, conv_w, key_dim, key_dim, l2norm=True)
        v = _conv_act(pp, conv_w, 2 * key_dim, val_dim, l2norm=False)
        ba = pp[:, :, qkv_dim + val_dim:qkv_dim + val_dim + 4 * HV]
        gr = _gates(_gate_rows(ba, chunk), a_col, dt_col, kind_col, dir_col, chunk)
        return q, k, v, _chunk_major(gr, chunk)

    ql, kl, vl, grl = dn_inputs(p)
    qc, kc, vc, grc = dn_inputs(pc)
    y = _delta(qc, kc, vc, grc, ql, kl, vl, grl, p, qkv_dim, dn_norm_g[0], chunk)
    h = _matmul(y.reshape(t_lat, val_dim), dn_w_out[0].astype(BF16), out_dtype=F32, tm=1024,
                tn=512, modv=modv(0, 1), resid=h, name="dn_out_proj")
    h = _ffn(h, modv(0, 2), norm_g[0, 2], *w_ffn, 0, 1)

    h = _ffn(h, modv(1, 0), norm_g[1, 0], *w_ffn, 1, 0)
    gdim = d // N_FGROUPS
    cos_c, sin_c = _dft_tables(gdim)
    w_cs = _chan_dft_weights(jnp.stack([cos_c, sin_c]), fn_w_out[0].astype(BF16))
    ab = _matmul(h, w_cs, out_dtype=BF16, tm=1024, tn=1024, modv=modv(1, 1), norm_g=norm_g[1, 1],
                 name="fourier_in").reshape(bsz, seq, 2 * d)
    cos_l, sin_l = _dft_tables(seq)
    h = _seq_dft(cos_l, sin_l, ab, modv(1, 1), h.reshape(bsz, seq, d),
                 1.0 / math.sqrt(seq * gdim)).reshape(t_lat, d)
    h = _ffn(h, modv(1, 2), norm_g[1, 2], *w_ffn, 1, 1, final_g=final_norm_g)
    return h.reshape(bsz, seq, d)
```

```python
import functools
import math

import jax
import jax.numpy as jnp
from jax import lax
from jax.experimental import pallas as pl
from jax.experimental.pallas import tpu as pltpu

F32 = jnp.float32
BF16 = jnp.bfloat16

EPS = 1e-6
FFN_RES = 0.5
N_SUB = 3
HK = 16
HV = 32
DK = 128
DV = 128
CONV_W = 5
N_FGROUPS = 4
DELTA_CHUNK = 128
PHASE_A_GROUP = 4

LANES = 128
MIB = 1024 * 1024


def _cparams(semantics, vmem_mib):
    return pltpu.CompilerParams(dimension_semantics=semantics,
                                vmem_limit_bytes=int(vmem_mib * MIB))


def _dot(a, b):
    return jnp.dot(a, b, preferred_element_type=F32)


def _dot_nt(a, b):
    return lax.dot_general(a, b, (((1,), (1,)), ((), ())), preferred_element_type=F32)


def _dot_tn(a, b):
    return lax.dot_general(a, b, (((0,), (0,)), ((), ())), preferred_element_type=F32)


def _silu(x):
    return x * jax.nn.sigmoid(x)


_NORM_ROWS = 32


def _ada_norm_into(h_ref, x_ref, g_ref, mod_ref):
    gain = g_ref[...] * (1.0 + mod_ref[0, 1:2, :])
    shift = mod_ref[0, 0:1, :]
    rows = min(_NORM_ROWS, x_ref.shape[0])

    def step(t, carry):
        r0 = pl.multiple_of(t * rows, rows)
        x = x_ref[pl.ds(r0, rows), :]
        ms = jnp.mean(x * x, axis=-1, keepdims=True)
        h_ref[pl.ds(r0, rows), :] = (x * lax.rsqrt(ms + EPS) * gain + shift).astype(BF16)
        return carry

    lax.fori_loop(0, x_ref.shape[0] // rows, step, 0, unroll=2)


def _mod_kernel(cond_ref, w_ref, b_ref, o_ref):
    s = _silu(cond_ref[...]).astype(BF16)
    o_ref[0] = _dot(s, w_ref[0].astype(BF16)) + b_ref[0]


def _modulation(cond, mod_w, mod_b):
    depth, d, n = mod_w.shape
    tn = 1024
    return pl.pallas_call(
        _mod_kernel,
        out_shape=jax.ShapeDtypeStruct((depth, 8, n), F32),
        grid=(depth, n // tn),
        in_specs=[pl.BlockSpec((8, d), lambda l, j: (0, 0)),
                  pl.BlockSpec((1, d, tn), lambda l, j: (l, 0, j)),
                  pl.BlockSpec((1, 1, tn), lambda l, j: (l, 0, j))],
        out_specs=pl.BlockSpec((1, 8, tn), lambda l, j: (l, 0, j)),
        compiler_params=_cparams(("parallel", "parallel"), 40),
        name="modulation",
    )(cond, mod_w, mod_b.reshape(depth, 1, n))


def _ffn_kernel(x_ref, mod_ref, g_ref, wg_ref, wu_ref, wd_ref, fg_ref, o_ref, h_ref, *,
                final_norm):
    j = pl.program_id(1)
    nj = pl.num_programs(1)

    @pl.when(j == 0)
    def _():
        _ada_norm_into(h_ref, x_ref, g_ref, mod_ref)
        o_ref[...] = jnp.zeros_like(o_ref)

    h = h_ref[...]
    a = (_silu(_dot(h, wg_ref[...].astype(BF16)))
         * _dot(h, wu_ref[...].astype(BF16))).astype(BF16)
    o_ref[...] += _dot(a, wd_ref[...].astype(BF16))

    @pl.when(j == nj - 1)
    def _():
        gate = FFN_RES * mod_ref[0, 2:3, :]
        rows = min(_NORM_ROWS, x_ref.shape[0])

        def step(t, carry):
            r0 = pl.multiple_of(t * rows, rows)
            r = x_ref[pl.ds(r0, rows), :] + gate * o_ref[pl.ds(r0, rows), :]
            if final_norm:
                ms = jnp.mean(r * r, axis=-1, keepdims=True)
                r = r * lax.rsqrt(ms + EPS) * fg_ref[...]
            o_ref[pl.ds(r0, rows), :] = r
            return carry

        lax.fori_loop(0, x_ref.shape[0] // rows, step, 0, unroll=2)


def _ffn(x, modv, g, wg, wu, wd, layer, which, final_g=None):
    t, d = x.shape
    f = wg.shape[3]
    tm = min(1024, t // modv.shape[0])
    tf = 256
    tiles_per_mod = t // tm // modv.shape[0]
    fg = (final_g if final_g is not None else g).reshape(1, d)
    return pl.pallas_call(
        functools.partial(_ffn_kernel, final_norm=final_g is not None),
        out_shape=jax.ShapeDtypeStruct((t, d), F32),
        grid=(t // tm, f // tf),
        in_specs=[pl.BlockSpec((tm, d), lambda i, j: (i, 0)),
                  pl.BlockSpec((1, 8, d), lambda i, j: (i // tiles_per_mod, 0, 0)),
                  pl.BlockSpec((1, d), lambda i, j: (0, 0)),
                  pl.BlockSpec((None, None, d, tf), lambda i, j: (layer, which, 0, j)),
                  pl.BlockSpec((None, None, d, tf), lambda i, j: (layer, which, 0, j)),
                  pl.BlockSpec((None, None, tf, d), lambda i, j: (layer, which, j, 0)),
                  pl.BlockSpec((1, d), lambda i, j: (0, 0))],
        out_specs=pl.BlockSpec((tm, d), lambda i, j: (i, 0)),
        scratch_shapes=[pltpu.VMEM((tm, d), BF16)],
        compiler_params=_cparams(("parallel", "arbitrary"), 56),
        name="ffn_half",
    )(x, modv, g.reshape(1, d), wg, wu, wd, fg)


def _mm_kernel(*refs, norm, resid):
    it = iter(refs)
    a_ref = next(it)
    b_ref = next(it)
    if norm:
        mod_ref, g_ref = next(it), next(it)
    elif resid:
        mod_ref = next(it)
    if resid:
        r_ref = next(it)
    o_ref = next(it)

    if norm:
        h_ref = next(it)

        @pl.when(pl.program_id(1) == 0)
        def _():
            _ada_norm_into(h_ref, a_ref, g_ref, mod_ref)
        a = h_ref[...]
    else:
        a = a_ref[...]
    acc = _dot(a, b_ref[...])
    if resid:
        acc = r_ref[...] + mod_ref[0, 2:3, :] * acc
    o_ref[...] = acc.astype(o_ref.dtype)


def _matmul(a, b, *, out_dtype, tm, tn, modv=None, norm_g=None, resid=None, vmem_mib=48,
            name="matmul"):
    m, k = a.shape
    n = b.shape[1]
    tm = min(tm, m if modv is None else m // modv.shape[0])
    norm = norm_g is not None
    has_res = resid is not None
    in_specs = [pl.BlockSpec((tm, k), lambda i, j: (i, 0)),
                pl.BlockSpec((k, tn), lambda i, j: (0, j))]
    args = [a, b]
    if norm or has_res:
        tiles_per_mod = m // tm // modv.shape[0]
        if norm:
            in_specs.append(pl.BlockSpec((1, 8, k), lambda i, j: (i // tiles_per_mod, 0, 0)))
            in_specs.append(pl.BlockSpec((1, k), lambda i, j: (0, 0)))
            args += [modv, norm_g.reshape(1, k)]
        else:
            in_specs.append(pl.BlockSpec((1, 8, tn), lambda i, j: (i // tiles_per_mod, 0, j)))
            args.append(modv)
    if has_res:
        in_specs.append(pl.BlockSpec((tm, tn), lambda i, j: (i, j)))
        args.append(resid)
    return pl.pallas_call(
        functools.partial(_mm_kernel, norm=norm, resid=has_res),
        out_shape=jax.ShapeDtypeStruct((m, n), out_dtype),
        grid=(m // tm, pl.cdiv(n, tn)),
        in_specs=in_specs,
        out_specs=pl.BlockSpec((tm, tn), lambda i, j: (i, j)),
        scratch_shapes=[pltpu.VMEM((tm, k), BF16)] if norm else [],
        compiler_params=_cparams(("parallel", "arbitrary"), vmem_mib),
        name=name,
    )(*args)


_CONV_ROWS = 256
_CONV_HALO = 8


def _conv_kernel(x_ref, w_ref, o_ref, xp_ref, *, seq, tc, l2norm, out_scale):
    zeros = jnp.zeros((_CONV_HALO, tc), F32)
    xp_ref[0:_CONV_HALO, :] = zeros
    xp_ref[_CONV_HALO + seq:, :] = zeros
    xp_ref[_CONV_HALO:_CONV_HALO + seq, :] = x_ref[0]
    w = w_ref[...]
    rows = min(_CONV_ROWS, seq)

    def step(t):
        base = t * rows
        acc = None
        for j in range(CONV_W):
            start = base + _CONV_HALO - CONV_W // 2 + j
            term = xp_ref[start:start + rows, :] * w[j:j + 1, :]
            acc = term if acc is None else acc + term
        y = _silu(acc)
        if l2norm:
            parts = []
            for hh in range(tc // LANES):
                yh = y[:, hh * LANES:(hh + 1) * LANES]
                ss = jnp.sum(yh * yh, axis=-1, keepdims=True)
                parts.append(yh * (lax.rsqrt(ss + EPS) * out_scale))
            y = parts[0] if len(parts) == 1 else jnp.concatenate(parts, axis=-1)
        o_ref[0, base:base + rows, :] = y.astype(o_ref.dtype)

    for t in range(seq // rows):
        step(t)


def _conv_act(p, conv_w, col0, ncols, *, l2norm, out_scale=1.0):
    b, s, _ = p.shape
    tc = 256
    off = col0 // tc
    return pl.pallas_call(
        functools.partial(_conv_kernel, seq=s, tc=tc, l2norm=l2norm, out_scale=out_scale),
        out_shape=jax.ShapeDtypeStruct((b, s, ncols), BF16),
        grid=(b, ncols // tc),
        in_specs=[pl.BlockSpec((1, s, tc), lambda i, j: (i, 0, j + off)),
                  pl.BlockSpec((CONV_W, tc), lambda i, j: (0, j + off))],
        out_specs=pl.BlockSpec((1, s, tc), lambda i, j: (i, 0, j)),
        scratch_shapes=[pltpu.VMEM((s + 2 * _CONV_HALO, tc), F32)],
        compiler_params=_cparams(("parallel", "parallel"), 40),
        name="dn_conv",
    )(p, conv_w)


def _split3(x):
    x1 = x.astype(BF16)
    r1 = x - x1.astype(F32)
    x2 = r1.astype(BF16)
    x3 = (r1 - x2.astype(F32)).astype(BF16)
    return x1, x2, x3


def _gates_kernel(ba_ref, a_ref, dt_ref, kind_ref, dir_ref, o_ref, *, chunk, tl):
    x = ba_ref[0]
    beta = jax.nn.sigmoid(x)
    y = x + dt_ref[...]
    softplus = jnp.maximum(y, 0.0) + jnp.log(1.0 + jnp.exp(-jnp.abs(y)))
    g = -jnp.exp(a_ref[...]) * softplus
    i = lax.broadcasted_iota(jnp.int32, (tl, tl), 0)
    j = lax.broadcasted_iota(jnp.int32, (tl, tl), 1)
    sh = int(math.log2(chunk))
    same = lax.shift_right_logical(i, sh) == lax.shift_right_logical(j, sh)
    pre = jnp.where(same & (i <= j), 1.0, 0.0).astype(BF16)
    suf = jnp.where(same & (i >= j), 1.0, 0.0).astype(BF16)
    g1, g2, g3 = _split3(g)
    cpre = _dot(g1, pre) + _dot(g2, pre) + _dot(g3, pre)
    csuf = _dot(g1, suf) + _dot(g2, suf) + _dot(g3, suf)
    cum = jnp.where(dir_ref[...] > 0.5, csuf, cpre)
    o_ref[0] = jnp.where(kind_ref[...] > 0.5, cum, beta)


def _gates(ba_t, a_col, dt_col, kind_col, dir_col, chunk):
    b, r, s = ba_t.shape
    tl = min(512, s)
    col = pl.BlockSpec((r, 1), lambda i, j: (0, 0))
    return pl.pallas_call(
        functools.partial(_gates_kernel, chunk=chunk, tl=tl),
        out_shape=jax.ShapeDtypeStruct((b, r, s), F32),
        grid=(b, s // tl),
        in_specs=[pl.BlockSpec((1, r, tl), lambda i, j: (i, 0, j)), col, col, col, col],
        out_specs=pl.BlockSpec((1, r, tl), lambda i, j: (i, 0, j)),
        compiler_params=_cparams(("parallel", "parallel"), 32),
        name="dn_gates",
    )(ba_t, a_col, dt_col, kind_col, dir_col)


def _col_form(gr_ref, dst_ref, n_rows, chunk):
    pad = jnp.zeros((LANES - 8, LANES), F32)
    for s in range(n_rows // LANES):
        if chunk <= LANES:
            per = LANES // chunk
            parts = [gr_ref[0, 0, s * per + t] for t in range(per)]
            slab = parts[0] if per == 1 else jnp.concatenate(parts, axis=-1)
        else:
            per = chunk // LANES
            slab = gr_ref[0, 0, s // per, :, (s % per) * LANES:(s % per + 1) * LANES]
        dst_ref[s * LANES:(s + 1) * LANES, :] = jnp.concatenate([slab, pad], axis=0).T


def _delta_kernel(qc_ref, kc_ref, vc_ref, grc_ref, ql_ref, kl_ref, vl_ref, grl_ref,
                  z_ref, ng_ref, o_ref, colc_ref, coll_ref, aq_ref, b_ref, gl_ref, s_ref,
                  oacc_ref, *, chunk, seq_c, seq_l):
    C = chunk
    n_c, n_l = seq_c // C, seq_l // C
    _col_form(grc_ref, colc_ref, seq_c, C)
    _col_form(grl_ref, coll_ref, seq_l, C)
    s_ref[...] = jnp.zeros_like(s_ref)
    oacc_ref[...] = jnp.zeros_like(oacc_ref)

    row = lax.broadcasted_iota(jnp.int32, (C, C), 0)
    col = lax.broadcasted_iota(jnp.int32, (C, C), 1)
    eye = jnp.where(row == col, 1.0, 0.0)

    def same_block(n):
        return lax.shift_right_logical(row, n) == lax.shift_right_logical(col, n)

    pair_mask = jnp.where(same_block(1), 1.0, 0.0)
    join_masks = [jnp.where(same_block(n + 1), 1.0, 0.0) - jnp.where(same_block(n), 1.0, 0.0)
                  for n in range(1, int(math.log2(C)))]

    chains = range(4)

    def take_rows(a, blk, d):
        off = blk if d == 0 else 0
        return jnp.concatenate([a[s + off:s + off + blk] for s in range(0, C, 2 * blk)], axis=0)

    def put_rows(a, new, blk, d):
        parts = []
        for i, s in enumerate(range(0, C, 2 * blk)):
            upd = new[i * blk:(i + 1) * blk]
            parts += [a[s:s + blk], upd] if d == 0 else [upd, a[s + blk:s + 2 * blk]]
        return jnp.concatenate(parts, axis=0)

    def phase_a(q_ref, k_ref, v_ref, gr_ref, col_ref, slot0, want_out, group):
        items = [(g, c) for g in range(group) for c in chains]

        def body(it, carry):
            cis = [it * group + g for g in range(group)]
            r0s = [pl.multiple_of(ci * C, C) for ci in cis]
            kb = [k_ref[0, pl.ds(r0, C), :] for r0 in r0s]
            kf = [t.astype(F32) for t in kb]
            gram = [_dot_nt(t, t) for t in kb]
            if want_out:
                qb = [q_ref[0, pl.ds(r0, C), :] for r0 in r0s]
                qk = [_dot_nt(qb[g], kb[g]) for g in range(group)]
            cols = [col_ref[pl.ds(r0, C), :] for r0 in r0s]
            beta, gcol, decay, lm, x = [], [], [], [], []
            for n, (g, c) in enumerate(items):
                d = c // 2
                beta.append(cols[g][:, c:c + 1])
                gcol.append(cols[g][:, 4 + c:5 + c])
                grow = gr_ref[0, 0, cis[g], 4 + c:5 + c, :]
                incl = (row >= col) if d == 0 else (row <= col)
                strict = (row > col) if d == 0 else (row < col)
                decay.append(jnp.exp(jnp.where(incl, gcol[n] - grow, -jnp.inf)))
                lm.append(jnp.where(strict, gram[g] * decay[n] * beta[n], 0.0))
                x.append(eye - lm[n] * pair_mask)
            idx = range(len(items))
            for lvl, jm in enumerate(join_masks, start=1):
                blk = 1 << lvl
                if blk < 8:
                    y = [_dot((lm[n] * jm).astype(BF16), x[n].astype(BF16)) for n in idx]
                    x = [x[n] - _dot(x[n].astype(BF16), y[n].astype(BF16)) for n in idx]
                    continue
                jmh = [take_rows(jm, blk, d) for d in range(2)]
                zeros = jnp.zeros((blk, C), F32)
                y = [_dot((take_rows(lm[n], blk, items[n][1] // 2)
                           * jmh[items[n][1] // 2]).astype(BF16), x[n].astype(BF16)) for n in idx]
                new = []
                for n in idx:
                    d = items[n][1] // 2
                    yfull = jnp.concatenate(
                        [t for s in range(C // (2 * blk))
                         for t in ((zeros, y[n][s * blk:(s + 1) * blk]) if d == 0
                                   else (y[n][s * blk:(s + 1) * blk], zeros))], axis=0)
                    xa = take_rows(x[n], blk, d)
                    new.append(xa - _dot(xa.astype(BF16), yfull.astype(BF16)))
                x = [put_rows(x[n], new[n], blk, items[n][1] // 2) for n in idx]
            eg = [jnp.exp(gcol[n]) for n in idx]
            sol = []
            for n, (g, c) in enumerate(items):
                e = c % 2
                vf = v_ref[0, pl.ds(r0s[g], C), e * DV:(e + 1) * DV].astype(F32)
                rhs = jnp.concatenate([vf * beta[n], kf[g] * (beta[n] * eg[n])], axis=-1)
                sol.append(_dot(x[n].astype(BF16), rhs.astype(BF16)).astype(BF16))
            kw = []
            for n, (g, c) in enumerate(items):
                glast = gcol[n][C - 1:C, :] if c // 2 == 0 else gcol[n][0:1, :]
                ktail = (kf[g] * jnp.exp(glast - gcol[n])).astype(BF16)
                kw.append(_dot_tn(ktail, sol[n]))
                gl_ref[c, slot0 + cis[g]] = jnp.broadcast_to(jnp.exp(glast), (8, LANES))
            if want_out:
                aw = [_dot((qk[g] * decay[n]).astype(BF16), sol[n])
                      for n, (g, c) in enumerate(items)]
            for n, (g, c) in enumerate(items):
                slot = slot0 + cis[g]
                b_ref[c, slot] = kw[n][:, :DV]
                aq_ref[c, slot, 0:DK, :] = (-kw[n][:, DV:]).astype(BF16)
                if want_out:
                    aq_ref[c, slot, DK:DK + C, :] = (qb[g].astype(F32) * eg[n]
                                                     - aw[n][:, DV:]).astype(BF16)
            if want_out:
                for g in range(group):
                    for e in range(2):
                        oacc_ref[pl.ds(r0s[g], C), e * DV:(e + 1) * DV] += (
                            aw[4 * g + e][:, :DV] + aw[4 * g + 2 + e][:, :DV])
            return carry
        return body

    def phase_b(n, slot0, want_out):
        def body(i, carry):
            states = [s_ref[c] for c in chains]
            cis = [i if c // 2 == 0 else n - 1 - i for c in chains]
            res = []
            for c in chains:
                slot = slot0 + cis[c]
                lhs = aq_ref[c, slot] if want_out else aq_ref[c, slot, 0:DK, :]
                res.append(_dot(lhs, states[c].astype(BF16)))
            for c in chains:
                slot = slot0 + cis[c]
                s_ref[c] = states[c] * gl_ref[c, slot, 0:1, :] + res[c][:DK] + b_ref[c, slot]
                if want_out:
                    r0 = pl.multiple_of(cis[c] * C, C)
                    e = c % 2
                    oacc_ref[pl.ds(r0, C), e * DV:(e + 1) * DV] += res[c][DK:]
            return carry
        return body

    grp_c = math.gcd(n_c, PHASE_A_GROUP)
    grp_l = math.gcd(n_l, PHASE_A_GROUP)
    lax.fori_loop(0, n_c // grp_c,
                  phase_a(qc_ref, kc_ref, vc_ref, grc_ref, colc_ref, 0, False, grp_c), 0)
    lax.fori_loop(0, n_l // grp_l,
                  phase_a(ql_ref, kl_ref, vl_ref, grl_ref, coll_ref, n_c, True, grp_l), 0)
    lax.fori_loop(0, n_c, phase_b(n_c, 0, False), 0)
    lax.fori_loop(0, n_l, phase_b(n_l, n_c, True), 0, unroll=4)

    rows = min(512, seq_l)

    def epilogue(t, carry):
        base = pl.multiple_of(t * rows, 8)
        for e in range(2):
            o = oacc_ref[pl.ds(base, rows), e * DV:(e + 1) * DV]
            ms = jnp.mean(o * o, axis=-1, keepdims=True)
            y = o * lax.rsqrt(ms + EPS) * ng_ref[...]
            zz = z_ref[0, pl.ds(base, rows), e * DV:(e + 1) * DV]
            o_ref[0, pl.ds(base, rows), e * DV:(e + 1) * DV] = (y * _silu(zz)).astype(o_ref.dtype)
        return carry

    lax.fori_loop(0, seq_l // rows, epilogue, 0)


def _delta(qc, kc, vc, grc, ql, kl, vl, grl, p_lat, z_col0, norm_g, chunk):
    b, seq_c, _ = qc.shape
    seq_l = ql.shape[1]
    n_c, n_l = seq_c // chunk, seq_l // chunk
    zoff = z_col0 // (2 * DV)

    def tok(seq, width):
        return pl.BlockSpec((1, seq, width), lambda i, h: (i, 0, h))

    def gate_spec(n):
        return pl.BlockSpec((1, 1, n, 8, chunk), lambda i, h: (i, h, 0, 0, 0))

    return pl.pallas_call(
        functools.partial(_delta_kernel, chunk=chunk, seq_c=seq_c, seq_l=seq_l),
        out_shape=jax.ShapeDtypeStruct((b, seq_l, HV * DV), BF16),
        grid=(b, HK),
        in_specs=[tok(seq_c, DK), tok(seq_c, DK), tok(seq_c, 2 * DV), gate_spec(n_c),
                  tok(seq_l, DK), tok(seq_l, DK), tok(seq_l, 2 * DV), gate_spec(n_l),
                  pl.BlockSpec((1, seq_l, 2 * DV), lambda i, h: (i, 0, h + zoff)),
                  pl.BlockSpec((1, DV), lambda i, h: (0, 0))],
        out_specs=tok(seq_l, 2 * DV),
        scratch_shapes=[pltpu.VMEM((seq_c, LANES), F32),
                        pltpu.VMEM((seq_l, LANES), F32),
                        pltpu.VMEM((4, n_c + n_l, DK + chunk, DV), BF16),
                        pltpu.VMEM((4, n_c + n_l, DK, DV), F32),
                        pltpu.VMEM((4, n_c + n_l, 8, LANES), F32),
                        pltpu.VMEM((4, DK, DV), F32),
                        pltpu.VMEM((seq_l, 2 * DV), F32)],
        compiler_params=_cparams(("parallel", "parallel"), 56),
        name="dn_delta",
    )(qc, kc, vc, grc, ql, kl, vl, grl, p_lat, norm_g.reshape(1, DV))


def _chan_dft_kernel(tab_ref, w_ref, o_ref):
    o_ref[...] = _dot(tab_ref[0], w_ref[...]).astype(o_ref.dtype)


def _chan_dft_weights(tabs, w):
    d = w.shape[0]
    g = tabs.shape[1]
    tn = 512
    nj = d // tn
    return pl.pallas_call(
        _chan_dft_kernel,
        out_shape=jax.ShapeDtypeStruct((d, 2 * d), BF16),
        grid=(d // g, 2, nj),
        in_specs=[pl.BlockSpec((1, g, g), lambda i, t, j: (t, 0, 0)),
                  pl.BlockSpec((g, tn), lambda i, t, j: (i, j))],
        out_specs=pl.BlockSpec((g, tn), lambda i, t, j: (i, t * nj + j)),
        compiler_params=_cparams(("parallel", "parallel", "parallel"), 32),
        name="fourier_chan",
    )(tabs, w)


def _dft_fold_kernel(lo_ref, hi_ref, ev_ref, od_ref, *, n, tr, d):
    lo = lo_ref[0].astype(F32)
    hi = hi_ref[0].astype(F32)
    ev_ref[0] = (lo + hi).astype(ev_ref.dtype)
    diff = lo - hi
    ad, bd = diff[:, :d], diff[:, d:]
    pos = pl.program_id(1) * tr + lax.broadcasted_iota(jnp.int32, (tr, 1), 0)
    theta = pos.astype(F32) * (2.0 * math.pi / n)
    c, s = jnp.cos(theta), jnp.sin(theta)
    od_ref[0, :, :d] = (ad * c - bd * s).astype(od_ref.dtype)
    od_ref[0, :, d:] = (bd * c + ad * s).astype(od_ref.dtype)


def _dft_fold(ab):
    b, n, d2 = ab.shape
    half = n // 2
    tr = min(256, half)
    nb = half // tr
    out = jax.ShapeDtypeStruct((b, half, d2), ab.dtype)
    spec = pl.BlockSpec((1, tr, d2), lambda i, t: (i, t, 0))
    return pl.pallas_call(
        functools.partial(_dft_fold_kernel, n=n, tr=tr, d=d2 // 2),
        out_shape=(out, out),
        grid=(b, nb),
        in_specs=[spec, pl.BlockSpec((1, tr, d2), lambda i, t: (i, t + nb, 0))],
        out_specs=(spec, spec),
        compiler_params=_cparams(("parallel", "parallel"), 40),
        name="fourier_fold",
    )(ab, ab)


def _seq_dft_kernel(cos_ref, sin_ref, ae_ref, be_ref, ao_ref, bo_ref, mod_ref, r_ref, o_ref,
                    mix_ref, *, scale, tm):
    cos_t, sin_t = cos_ref[...], sin_ref[...]
    gate = mod_ref[0, 2:3, :] * scale
    even = gate * (_dot(cos_t, ae_ref[0]) - _dot(sin_t, be_ref[0]))
    odd = gate * (_dot(cos_t, ao_ref[0]) - _dot(sin_t, bo_ref[0]))
    for q in range(mix_ref.shape[0]):
        lanes = slice(q * LANES, (q + 1) * LANES)
        mix_ref[q, pl.ds(0, tm, stride=2), :] = even[:, lanes]
        mix_ref[q, pl.ds(1, tm, stride=2), :] = odd[:, lanes]
        o_ref[0, :, lanes] = r_ref[0, :, lanes] + mix_ref[q]


def _seq_dft(cos_t, sin_t, ab_even, ab_odd, modv, resid, scale):
    b, half, d2 = ab_even.shape
    d = d2 // 2
    tm, tn = min(512, half), 512
    nj = d // tn
    tab = pl.BlockSpec((tm, half), lambda n, i, j: (i, 0))
    a_spec = pl.BlockSpec((1, half, tn), lambda n, i, j: (n, 0, j))
    b_spec = pl.BlockSpec((1, half, tn), lambda n, i, j: (n, 0, j + nj))
    io = pl.BlockSpec((1, 2 * tm, tn), lambda n, i, j: (n, i, j))
    return pl.pallas_call(
        functools.partial(_seq_dft_kernel, scale=scale, tm=tm),
        out_shape=jax.ShapeDtypeStruct((b, 2 * half, d), F32),
        grid=(b, half // tm, nj),
        in_specs=[tab, tab, a_spec, b_spec, a_spec, b_spec,
                  pl.BlockSpec((1, 8, tn), lambda n, i, j: (n, 0, j)), io],
        out_specs=io,
        scratch_shapes=[pltpu.VMEM((tn // LANES, 2 * tm, LANES), F32)],
        compiler_params=_cparams(("parallel", "parallel", "arbitrary"), 48),
        name="fourier_seq",
    )(cos_t, sin_t, ab_even, ab_even, ab_odd, ab_odd, modv, resid)


def _dft_table_kernel(cos_ref, sin_ref, tc_ref, ts_ref, *, n, tk):
    w = 2.0 * math.pi / n

    def unit(phase):
        ang = (phase & (n - 1)).astype(F32) * w
        return jnp.cos(ang), jnp.sin(ang)

    @pl.when(pl.program_id(0) == 0)
    def _():
        a = lax.broadcasted_iota(jnp.int32, (tk, n), 0)
        m = lax.broadcasted_iota(jnp.int32, (tk, n), 1)
        tc_ref[...], ts_ref[...] = unit(a * m)

    m1 = lax.broadcasted_iota(jnp.int32, (1, n), 1)
    rc, rs = unit((pl.program_id(0) * tk) * m1)
    tcv, tsv = tc_ref[...], ts_ref[...]
    cos_ref[...] = (rc * tcv - rs * tsv).astype(cos_ref.dtype)
    sin_ref[...] = (rs * tcv + rc * tsv).astype(sin_ref.dtype)


def _dft_tables(n):
    tk = min(256, n)
    out = jax.ShapeDtypeStruct((n, n), BF16)
    spec = pl.BlockSpec((tk, n), lambda i: (i, 0))
    return pl.pallas_call(
        functools.partial(_dft_table_kernel, n=n, tk=tk),
        out_shape=(out, out),
        grid=(n // tk,),
        in_specs=[],
        out_specs=(spec, spec),
        scratch_shapes=[pltpu.VMEM((tk, n), F32), pltpu.VMEM((tk, n), F32)],
        compiler_params=_cparams(("arbitrary",), 40),
        name="dft_tables",
    )()


def _gate_rows(p_ba, chunk):
    b, s, _ = p_ba.shape
    t = p_ba.reshape(b, s, 2, 2, HK, 2)
    return jnp.transpose(t, (0, 4, 2, 3, 5, 1)).reshape(b, 4 * HV, s)


def _gate_params(a_log, dt_bias):
    def rows(v, kind_fill):
        t = jnp.stack([jnp.full_like(v, kind_fill), v]).reshape(2, 2, HK, 2)
        return jnp.transpose(t, (2, 0, 1, 3)).reshape(4 * HV, 1)
    kind = rows(jnp.ones((2, HV), F32), 0.0)
    dirn = rows(jnp.stack([jnp.zeros((HV,), F32), jnp.ones((HV,), F32)]), 0.0)
    return rows(a_log.astype(F32), 0.0), rows(dt_bias.astype(F32), 0.0), kind, dirn


def _chunk_major(rows_t, chunk):
    b, _, s = rows_t.shape
    t = rows_t.reshape(b, HK, 8, s // chunk, chunk)
    return jnp.transpose(t, (0, 1, 3, 2, 4))


def kernel(x, c, ctx, c_ctx, norm_g, mod_w, mod_b, ffn_w_gate, ffn_w_up, ffn_w_down, dn_w_in,
           dn_conv_w, dn_a_log, dn_dt_bias, dn_norm_g, dn_w_out, fn_w_out, final_norm_g):
    bsz, seq, d = x.shape
    seq_c = ctx.shape[1]
    t_lat, t_ctx = bsz * seq, bsz * seq_c
    key_dim, val_dim = HK * DK, HV * DV
    qkv_dim = 2 * key_dim + val_dim
    chunk = DELTA_CHUNK

    cond = jnp.zeros((8, d), F32).at[:bsz].set(c).at[bsz].set(c_ctx)
    mod = _modulation(cond, mod_w, mod_b)

    def modv(layer, sub, ctx_rows=False):
        m3 = mod[layer, :, 3 * sub * d:3 * (sub + 1) * d].reshape(8, 3, d)
        m3 = m3[bsz:bsz + 1] if ctx_rows else m3[:bsz]
        return jnp.pad(m3, ((0, 0), (0, 5), (0, 0)))

    w_ffn = (ffn_w_gate, ffn_w_up, ffn_w_down)

    h = x.reshape(t_lat, d)
    hc = ctx.reshape(t_ctx, d)

    h = _ffn(h, modv(0, 0), norm_g[0, 0], *w_ffn, 0, 0)
    hc = _ffn(hc, modv(0, 0, True), norm_g[0, 0], *w_ffn, 0, 0)

    n_in = dn_w_in.shape[2]
    w_in = dn_w_in[0].astype(BF16)
    p = _matmul(h, w_in, out_dtype=F32, tm=1024, tn=512, modv=modv(0, 1), norm_g=norm_g[0, 1],
                name="dn_in_proj").reshape(bsz, seq, n_in)
    pc = _matmul(hc, w_in, out_dtype=F32, tm=512, tn=512, modv=modv(0, 1, True),
                 norm_g=norm_g[0, 1], name="dn_in_proj_ctx").reshape(bsz, seq_c, n_in)

    conv_w = dn_conv_w[0]
    a_col, dt_col, kind_col, dir_col = _gate_params(dn_a_log[0], dn_dt_bias[0])

    def dn_inputs(pp):
        q = _conv_act(pp, conv_w, 0, key_dim, l2norm=True, out_scale=DK ** -0.5)
        k = _conv_act(pp, conv_w, key_dim, key_dim, l2norm=True)
        v = _conv_act(pp, conv_w, 2 * key_dim, val_dim, l2norm=False)
        ba = pp[:, :, qkv_dim + val_dim:qkv_dim + val_dim + 4 * HV]
        gr = _gates(_gate_rows(ba, chunk), a_col, dt_col, kind_col, dir_col, chunk)
        return q, k, v, _chunk_major(gr, chunk)

    ql, kl, vl, grl = dn_inputs(p)
    qc, kc, vc, grc = dn_inputs(pc)
    y = _delta(qc, kc, vc, grc, ql, kl, vl, grl, p, qkv_dim, dn_norm_g[0], chunk)
    h = _matmul(y.reshape(t_lat, val_dim), dn_w_out[0].astype(BF16), out_dtype=F32, tm=1024,
                tn=512, modv=modv(0, 1), resid=h, name="dn_out_proj")
    h = _ffn(h, modv(0, 2), norm_g[0, 2], *w_ffn, 0, 1)

    h = _ffn(h, modv(1, 0), norm_g[1, 0], *w_ffn, 1, 0)
    gdim = d // N_FGROUPS
    cos_c, sin_c = _dft_tables(gdim)
    w_cs = _chan_dft_weights(jnp.stack([cos_c, sin_c]), fn_w_out[0].astype(BF16))
    ab = _matmul(h, w_cs, out_dtype=BF16, tm=1024, tn=1024, modv=modv(1, 1), norm_g=norm_g[1, 1],
                 name="fourier_in").reshape(bsz, seq, 2 * d)
    cos_l, sin_l = _dft_tables(seq // 2)
    ab_even, ab_odd = _dft_fold(ab)
    h = _seq_dft(cos_l, sin_l, ab_even, ab_odd, modv(1, 1), h.reshape(bsz, seq, d),
                 1.0 / math.sqrt(seq * gdim)).reshape(t_lat, d)
    h = _ffn(h, modv(1, 2), norm_g[1, 2], *w_ffn, 1, 1, final_g=final_norm_g)
    return h.reshape(bsz, seq, d)
```

```python
import functools
import math

import jax
import jax.numpy as jnp
from jax import lax
from jax.experimental import pallas as pl
from jax.experimental.pallas import tpu as pltpu

F32 = jnp.float32
BF16 = jnp.bfloat16

EPS = 1e-6
FFN_RES = 0.5
N_SUB = 3
HK = 16
HV = 32
DK = 128
DV = 128
CONV_W = 5
N_FGROUPS = 4
DELTA_CHUNK = 128
PHASE_A_GROUP = 4

LANES = 128
MIB = 1024 * 1024


def _cparams(semantics, vmem_mib):
    return pltpu.CompilerParams(dimension_semantics=semantics,
                                vmem_limit_bytes=int(vmem_mib * MIB))


def _dot(a, b):
    return jnp.dot(a, b, preferred_element_type=F32)


def _dot_nt(a, b):
    return lax.dot_general(a, b, (((1,), (1,)), ((), ())), preferred_element_type=F32)


def _dot_tn(a, b):
    return lax.dot_general(a, b, (((0,), (0,)), ((), ())), preferred_element_type=F32)


def _silu(x):
    return x * jax.nn.sigmoid(x)


_NORM_ROWS = 32


def _ada_norm_into(h_ref, x_ref, g_ref, mod_ref):
    gain = g_ref[...] * (1.0 + mod_ref[0, 1:2, :])
    shift = mod_ref[0, 0:1, :]
    rows = min(_NORM_ROWS, x_ref.shape[0])

    def step(t, carry):
        r0 = pl.multiple_of(t * rows, rows)
        x = x_ref[pl.ds(r0, rows), :]
        ms = jnp.mean(x * x, axis=-1, keepdims=True)
        h_ref[pl.ds(r0, rows), :] = (x * lax.rsqrt(ms + EPS) * gain + shift).astype(BF16)
        return carry

    lax.fori_loop(0, x_ref.shape[0] // rows, step, 0, unroll=2)


def _mod_kernel(cond_ref, w_ref, b_ref, o_ref):
    s = _silu(cond_ref[...]).astype(BF16)
    o_ref[0] = _dot(s, w_ref[0].astype(BF16)) + b_ref[0]


def _modulation(cond, mod_w, mod_b):
    depth, d, n = mod_w.shape
    tn = 1024
    return pl.pallas_call(
        _mod_kernel,
        out_shape=jax.ShapeDtypeStruct((depth, 8, n), F32),
        grid=(depth, n // tn),
        in_specs=[pl.BlockSpec((8, d), lambda l, j: (0, 0)),
                  pl.BlockSpec((1, d, tn), lambda l, j: (l, 0, j)),
                  pl.BlockSpec((1, 1, tn), lambda l, j: (l, 0, j))],
        out_specs=pl.BlockSpec((1, 8, tn), lambda l, j: (l, 0, j)),
        compiler_params=_cparams(("parallel", "parallel"), 40),
        name="modulation",
    )(cond, mod_w, mod_b.reshape(depth, 1, n))


def _ffn_kernel(x_ref, mod_ref, g_ref, wg_ref, wu_ref, wd_ref, fg_ref, o_ref, h_ref, *,
                final_norm):
    j = pl.program_id(1)
    nj = pl.num_programs(1)

    @pl.when(j == 0)
    def _():
        _ada_norm_into(h_ref, x_ref, g_ref, mod_ref)
        o_ref[...] = jnp.zeros_like(o_ref)

    h = h_ref[...]
    a = (_silu(_dot(h, wg_ref[...].astype(BF16)))
         * _dot(h, wu_ref[...].astype(BF16))).astype(BF16)
    o_ref[...] += _dot(a, wd_ref[...].astype(BF16))

    @pl.when(j == nj - 1)
    def _():
        gate = FFN_RES * mod_ref[0, 2:3, :]
        rows = min(_NORM_ROWS, x_ref.shape[0])

        def step(t, carry):
            r0 = pl.multiple_of(t * rows, rows)
            r = x_ref[pl.ds(r0, rows), :] + gate * o_ref[pl.ds(r0, rows), :]
            if final_norm:
                ms = jnp.mean(r * r, axis=-1, keepdims=True)
                r = r * lax.rsqrt(ms + EPS) * fg_ref[...]
            o_ref[pl.ds(r0, rows), :] = r
            return carry

        lax.fori_loop(0, x_ref.shape[0] // rows, step, 0, unroll=2)


def _ffn(x, modv, g, wg, wu, wd, layer, which, final_g=None):
    t, d = x.shape
    f = wg.shape[3]
    tm = min(1024, t // modv.shape[0])
    tf = 256
    tiles_per_mod = t // tm // modv.shape[0]
    fg = (final_g if final_g is not None else g).reshape(1, d)
    return pl.pallas_call(
        functools.partial(_ffn_kernel, final_norm=final_g is not None),
        out_shape=jax.ShapeDtypeStruct((t, d), F32),
        grid=(t // tm, f // tf),
        in_specs=[pl.BlockSpec((tm, d), lambda i, j: (i, 0)),
                  pl.BlockSpec((1, 8, d), lambda i, j: (i // tiles_per_mod, 0, 0)),
                  pl.BlockSpec((1, d), lambda i, j: (0, 0)),
                  pl.BlockSpec((None, None, d, tf), lambda i, j: (layer, which, 0, j)),
                  pl.BlockSpec((None, None, d, tf), lambda i, j: (layer, which, 0, j)),
                  pl.BlockSpec((None, None, tf, d), lambda i, j: (layer, which, j, 0)),
                  pl.BlockSpec((1, d), lambda i, j: (0, 0))],
        out_specs=pl.BlockSpec((tm, d), lambda i, j: (i, 0)),
        scratch_shapes=[pltpu.VMEM((tm, d), BF16)],
        compiler_params=_cparams(("parallel", "arbitrary"), 56),
        name="ffn_half",
    )(x, modv, g.reshape(1, d), wg, wu, wd, fg)


def _mm_kernel(*refs, norm, resid):
    it = iter(refs)
    a_ref = next(it)
    b_ref = next(it)
    if norm:
        mod_ref, g_ref = next(it), next(it)
    elif resid:
        mod_ref = next(it)
    if resid:
        r_ref = next(it)
    o_ref = next(it)

    if norm:
        h_ref = next(it)

        @pl.when(pl.program_id(1) == 0)
        def _():
            _ada_norm_into(h_ref, a_ref, g_ref, mod_ref)
        a = h_ref[...]
    else:
        a = a_ref[...]
    acc = _dot(a, b_ref[...])
    if resid:
        acc = r_ref[...] + mod_ref[0, 2:3, :] * acc
    o_ref[...] = acc.astype(o_ref.dtype)


def _matmul(a, b, *, out_dtype, tm, tn, modv=None, norm_g=None, resid=None, vmem_mib=48,
            lhs_buffers=2, name="matmul"):
    m, k = a.shape
    n = b.shape[1]
    tm = min(tm, m if modv is None else m // modv.shape[0])
    norm = norm_g is not None
    has_res = resid is not None
    lhs_mode = {} if lhs_buffers == 2 else {"pipeline_mode": pl.Buffered(lhs_buffers)}
    in_specs = [pl.BlockSpec((tm, k), lambda i, j: (i, 0), **lhs_mode),
                pl.BlockSpec((k, tn), lambda i, j: (0, j))]
    args = [a, b]
    if norm or has_res:
        tiles_per_mod = m // tm // modv.shape[0]
        if norm:
            in_specs.append(pl.BlockSpec((1, 8, k), lambda i, j: (i // tiles_per_mod, 0, 0)))
            in_specs.append(pl.BlockSpec((1, k), lambda i, j: (0, 0)))
            args += [modv, norm_g.reshape(1, k)]
        else:
            in_specs.append(pl.BlockSpec((1, 8, tn), lambda i, j: (i // tiles_per_mod, 0, j)))
            args.append(modv)
    if has_res:
        in_specs.append(pl.BlockSpec((tm, tn), lambda i, j: (i, j)))
        args.append(resid)
    return pl.pallas_call(
        functools.partial(_mm_kernel, norm=norm, resid=has_res),
        out_shape=jax.ShapeDtypeStruct((m, n), out_dtype),
        grid=(m // tm, pl.cdiv(n, tn)),
        in_specs=in_specs,
        out_specs=pl.BlockSpec((tm, tn), lambda i, j: (i, j)),
        scratch_shapes=[pltpu.VMEM((tm, k), BF16)] if norm else [],
        compiler_params=_cparams(("parallel", "arbitrary"), vmem_mib),
        name=name,
    )(*args)


_CONV_ROWS = 256
_CONV_HALO = 8


def _conv_kernel(x_ref, w_ref, o_ref, xp_ref, *, seq, tc, l2norm, out_scale):
    zeros = jnp.zeros((_CONV_HALO, tc), F32)
    xp_ref[0:_CONV_HALO, :] = zeros
    xp_ref[_CONV_HALO + seq:, :] = zeros
    xp_ref[_CONV_HALO:_CONV_HALO + seq, :] = x_ref[0]
    w = w_ref[...]
    rows = min(_CONV_ROWS, seq)

    def step(t):
        base = t * rows
        acc = None
        for j in range(CONV_W):
            start = base + _CONV_HALO - CONV_W // 2 + j
            term = xp_ref[start:start + rows, :] * w[j:j + 1, :]
            acc = term if acc is None else acc + term
        y = _silu(acc)
        if l2norm:
            parts = []
            for hh in range(tc // LANES):
                yh = y[:, hh * LANES:(hh + 1) * LANES]
                ss = jnp.sum(yh * yh, axis=-1, keepdims=True)
                parts.append(yh * (lax.rsqrt(ss + EPS) * out_scale))
            y = parts[0] if len(parts) == 1 else jnp.concatenate(parts, axis=-1)
        o_ref[0, base:base + rows, :] = y.astype(o_ref.dtype)

    for t in range(seq // rows):
        step(t)


def _conv_act(p, conv_w, col0, ncols, *, l2norm, out_scale=1.0):
    b, s, _ = p.shape
    tc = 256
    off = col0 // tc
    return pl.pallas_call(
        functools.partial(_conv_kernel, seq=s, tc=tc, l2norm=l2norm, out_scale=out_scale),
        out_shape=jax.ShapeDtypeStruct((b, s, ncols), BF16),
        grid=(b, ncols // tc),
        in_specs=[pl.BlockSpec((1, s, tc), lambda i, j: (i, 0, j + off)),
                  pl.BlockSpec((CONV_W, tc), lambda i, j: (0, j + off))],
        out_specs=pl.BlockSpec((1, s, tc), lambda i, j: (i, 0, j)),
        scratch_shapes=[pltpu.VMEM((s + 2 * _CONV_HALO, tc), F32)],
        compiler_params=_cparams(("parallel", "parallel"), 40),
        name="dn_conv",
    )(p, conv_w)


def _split3(x):
    x1 = x.astype(BF16)
    r1 = x - x1.astype(F32)
    x2 = r1.astype(BF16)
    x3 = (r1 - x2.astype(F32)).astype(BF16)
    return x1, x2, x3


def _gates_kernel(ba_ref, a_ref, dt_ref, kind_ref, dir_ref, o_ref, *, chunk, tl):
    x = ba_ref[0]
    beta = jax.nn.sigmoid(x)
    y = x + dt_ref[...]
    softplus = jnp.maximum(y, 0.0) + jnp.log(1.0 + jnp.exp(-jnp.abs(y)))
    g = -jnp.exp(a_ref[...]) * softplus
    i = lax.broadcasted_iota(jnp.int32, (tl, tl), 0)
    j = lax.broadcasted_iota(jnp.int32, (tl, tl), 1)
    sh = int(math.log2(chunk))
    same = lax.shift_right_logical(i, sh) == lax.shift_right_logical(j, sh)
    pre = jnp.where(same & (i <= j), 1.0, 0.0).astype(BF16)
    suf = jnp.where(same & (i >= j), 1.0, 0.0).astype(BF16)
    g1, g2, g3 = _split3(g)
    cpre = _dot(g1, pre) + _dot(g2, pre) + _dot(g3, pre)
    csuf = _dot(g1, suf) + _dot(g2, suf) + _dot(g3, suf)
    cum = jnp.where(dir_ref[...] > 0.5, csuf, cpre)
    o_ref[0] = jnp.where(kind_ref[...] > 0.5, cum, beta)


def _gates(ba_t, a_col, dt_col, kind_col, dir_col, chunk):
    b, r, s = ba_t.shape
    tl = min(512, s)
    col = pl.BlockSpec((r, 1), lambda i, j: (0, 0))
    return pl.pallas_call(
        functools.partial(_gates_kernel, chunk=chunk, tl=tl),
        out_shape=jax.ShapeDtypeStruct((b, r, s), F32),
        grid=(b, s // tl),
        in_specs=[pl.BlockSpec((1, r, tl), lambda i, j: (i, 0, j)), col, col, col, col],
        out_specs=pl.BlockSpec((1, r, tl), lambda i, j: (i, 0, j)),
        compiler_params=_cparams(("parallel", "parallel"), 32),
        name="dn_gates",
    )(ba_t, a_col, dt_col, kind_col, dir_col)


def _col_form(gr_ref, dst_ref, n_rows, chunk):
    pad = jnp.zeros((LANES - 8, LANES), F32)
    for s in range(n_rows // LANES):
        if chunk <= LANES:
            per = LANES // chunk
            parts = [gr_ref[0, 0, s * per + t] for t in range(per)]
            slab = parts[0] if per == 1 else jnp.concatenate(parts, axis=-1)
        else:
            per = chunk // LANES
            slab = gr_ref[0, 0, s // per, :, (s % per) * LANES:(s % per + 1) * LANES]
        dst_ref[s * LANES:(s + 1) * LANES, :] = jnp.concatenate([slab, pad], axis=0).T


def _delta_kernel(qc_ref, kc_ref, vc_ref, grc_ref, ql_ref, kl_ref, vl_ref, grl_ref,
                  z_ref, ng_ref, o_ref, colc_ref, coll_ref, aq_ref, b_ref, gl_ref, s_ref,
                  oacc_ref, *, chunk, seq_c, seq_l):
    C = chunk
    n_c, n_l = seq_c // C, seq_l // C
    _col_form(grc_ref, colc_ref, seq_c, C)
    _col_form(grl_ref, coll_ref, seq_l, C)
    s_ref[...] = jnp.zeros_like(s_ref)
    oacc_ref[...] = jnp.zeros_like(oacc_ref)

    row = lax.broadcasted_iota(jnp.int32, (C, C), 0)
    col = lax.broadcasted_iota(jnp.int32, (C, C), 1)
    eye = jnp.where(row == col, 1.0, 0.0)

    def same_block(n):
        return lax.shift_right_logical(row, n) == lax.shift_right_logical(col, n)

    pair_mask = jnp.where(same_block(1), 1.0, 0.0)
    join_masks = [jnp.where(same_block(n + 1), 1.0, 0.0) - jnp.where(same_block(n), 1.0, 0.0)
                  for n in range(1, int(math.log2(C)))]

    chains = range(4)

    def take_rows(a, blk, d):
        off = blk if d == 0 else 0
        return jnp.concatenate([a[s + off:s + off + blk] for s in range(0, C, 2 * blk)], axis=0)

    def put_rows(a, new, blk, d):
        parts = []
        for i, s in enumerate(range(0, C, 2 * blk)):
            upd = new[i * blk:(i + 1) * blk]
            parts += [a[s:s + blk], upd] if d == 0 else [upd, a[s + blk:s + 2 * blk]]
        return jnp.concatenate(parts, axis=0)

    def phase_a(q_ref, k_ref, v_ref, gr_ref, col_ref, slot0, want_out, group):
        items = [(g, c) for g in range(group) for c in chains]

        def body(it, carry):
            cis = [it * group + g for g in range(group)]
            r0s = [pl.multiple_of(ci * C, C) for ci in cis]
            kb = [k_ref[0, pl.ds(r0, C), :] for r0 in r0s]
            kf = [t.astype(F32) for t in kb]
            gram = [_dot_nt(t, t) for t in kb]
            if want_out:
                qb = [q_ref[0, pl.ds(r0, C), :] for r0 in r0s]
                qk = [_dot_nt(qb[g], kb[g]) for g in range(group)]
            cols = [col_ref[pl.ds(r0, C), :] for r0 in r0s]
            beta, gcol, decay, lm, x = [], [], [], [], []
            for n, (g, c) in enumerate(items):
                d = c // 2
                beta.append(cols[g][:, c:c + 1])
                gcol.append(cols[g][:, 4 + c:5 + c])
                grow = gr_ref[0, 0, cis[g], 4 + c:5 + c, :]
                incl = (row >= col) if d == 0 else (row <= col)
                strict = (row > col) if d == 0 else (row < col)
                decay.append(jnp.exp(jnp.where(incl, gcol[n] - grow, -jnp.inf)))
                lm.append(jnp.where(strict, gram[g] * decay[n] * beta[n], 0.0))
                x.append(eye - lm[n] * pair_mask)
            idx = range(len(items))
            for lvl, jm in enumerate(join_masks, start=1):
                blk = 1 << lvl
                if blk < 8:
                    y = [_dot((lm[n] * jm).astype(BF16), x[n].astype(BF16)) for n in idx]
                    x = [x[n] - _dot(x[n].astype(BF16), y[n].astype(BF16)) for n in idx]
                    continue
                jmh = [take_rows(jm, blk, d) for d in range(2)]
                zeros = jnp.zeros((blk, C), F32)
                y = [_dot((take_rows(lm[n], blk, items[n][1] // 2)
                           * jmh[items[n][1] // 2]).astype(BF16), x[n].astype(BF16)) for n in idx]
                new = []
                for n in idx:
                    d = items[n][1] // 2
                    yfull = jnp.concatenate(
                        [t for s in range(C // (2 * blk))
                         for t in ((zeros, y[n][s * blk:(s + 1) * blk]) if d == 0
                                   else (y[n][s * blk:(s + 1) * blk], zeros))], axis=0)
                    xa = take_rows(x[n], blk, d)
                    new.append(xa - _dot(xa.astype(BF16), yfull.astype(BF16)))
                x = [put_rows(x[n], new[n], blk, items[n][1] // 2) for n in idx]
            eg = [jnp.exp(gcol[n]) for n in idx]
            sol = []
            for n, (g, c) in enumerate(items):
                e = c % 2
                vf = v_ref[0, pl.ds(r0s[g], C), e * DV:(e + 1) * DV].astype(F32)
                rhs = jnp.concatenate([vf * beta[n], kf[g] * (beta[n] * eg[n])], axis=-1)
                sol.append(_dot(x[n].astype(BF16), rhs.astype(BF16)).astype(BF16))
            kw = []
            for n, (g, c) in enumerate(items):
                glast = gcol[n][C - 1:C, :] if c // 2 == 0 else gcol[n][0:1, :]
                ktail = (kf[g] * jnp.exp(glast - gcol[n])).astype(BF16)
                kw.append(_dot_tn(ktail, sol[n]))
                gl_ref[c, slot0 + cis[g]] = jnp.broadcast_to(jnp.exp(glast), (8, LANES))
            if want_out:
                aw = [_dot((qk[g] * decay[n]).astype(BF16), sol[n])
                      for n, (g, c) in enumerate(items)]
            for n, (g, c) in enumerate(items):
                slot = slot0 + cis[g]
                b_ref[c, slot] = kw[n][:, :DV]
                aq_ref[c, slot, 0:DK, :] = (-kw[n][:, DV:]).astype(BF16)
                if want_out:
                    aq_ref[c, slot, DK:DK + C, :] = (qb[g].astype(F32) * eg[n]
                                                     - aw[n][:, DV:]).astype(BF16)
            if want_out:
                for g in range(group):
                    for e in range(2):
                        oacc_ref[pl.ds(r0s[g], C), e * DV:(e + 1) * DV] += (
                            aw[4 * g + e][:, :DV] + aw[4 * g + 2 + e][:, :DV])
            return carry
        return body

    def phase_b(n, slot0, want_out):
        def body(i, carry):
            states = [s_ref[c] for c in chains]
            cis = [i if c // 2 == 0 else n - 1 - i for c in chains]
            res = []
            for c in chains:
                slot = slot0 + cis[c]
                lhs = aq_ref[c, slot] if want_out else aq_ref[c, slot, 0:DK, :]
                res.append(_dot(lhs, states[c].astype(BF16)))
            for c in chains:
                slot = slot0 + cis[c]
                s_ref[c] = states[c] * gl_ref[c, slot, 0:1, :] + res[c][:DK] + b_ref[c, slot]
                if want_out:
                    r0 = pl.multiple_of(cis[c] * C, C)
                    e = c % 2
                    oacc_ref[pl.ds(r0, C), e * DV:(e + 1) * DV] += res[c][DK:]
            return carry
        return body

    grp_c = math.gcd(n_c, PHASE_A_GROUP)
    grp_l = math.gcd(n_l, PHASE_A_GROUP)
    lax.fori_loop(0, n_c // grp_c,
                  phase_a(qc_ref, kc_ref, vc_ref, grc_ref, colc_ref, 0, False, grp_c), 0)
    lax.fori_loop(0, n_l // grp_l,
                  phase_a(ql_ref, kl_ref, vl_ref, grl_ref, coll_ref, n_c, True, grp_l), 0)
    lax.fori_loop(0, n_c, phase_b(n_c, 0, False), 0)
    lax.fori_loop(0, n_l, phase_b(n_l, n_c, True), 0, unroll=4)

    rows = min(512, seq_l)

    def epilogue(t, carry):
        base = pl.multiple_of(t * rows, 8)
        for e in range(2):
            o = oacc_ref[pl.ds(base, rows), e * DV:(e + 1) * DV]
            ms = jnp.mean(o * o, axis=-1, keepdims=True)
            y = o * lax.rsqrt(ms + EPS) * ng_ref[...]
            zz = z_ref[0, pl.ds(base, rows), e * DV:(e + 1) * DV]
            o_ref[0, pl.ds(base, rows), e * DV:(e + 1) * DV] = (y * _silu(zz)).astype(o_ref.dtype)
        return carry

    lax.fori_loop(0, seq_l // rows, epilogue, 0)


def _delta(qc, kc, vc, grc, ql, kl, vl, grl, p_lat, z_col0, norm_g, chunk):
    b, seq_c, _ = qc.shape
    seq_l = ql.shape[1]
    n_c, n_l = seq_c // chunk, seq_l // chunk
    zoff = z_col0 // (2 * DV)

    def tok(seq, width):
        return pl.BlockSpec((1, seq, width), lambda i, h: (i, 0, h))

    def gate_spec(n):
        return pl.BlockSpec((1, 1, n, 8, chunk), lambda i, h: (i, h, 0, 0, 0))

    return pl.pallas_call(
        functools.partial(_delta_kernel, chunk=chunk, seq_c=seq_c, seq_l=seq_l),
        out_shape=jax.ShapeDtypeStruct((b, seq_l, HV * DV), BF16),
        grid=(b, HK),
        in_specs=[tok(seq_c, DK), tok(seq_c, DK), tok(seq_c, 2 * DV), gate_spec(n_c),
                  tok(seq_l, DK), tok(seq_l, DK), tok(seq_l, 2 * DV), gate_spec(n_l),
                  pl.BlockSpec((1, seq_l, 2 * DV), lambda i, h: (i, 0, h + zoff)),
                  pl.BlockSpec((1, DV), lambda i, h: (0, 0))],
        out_specs=tok(seq_l, 2 * DV),
        scratch_shapes=[pltpu.VMEM((seq_c, LANES), F32),
                        pltpu.VMEM((seq_l, LANES), F32),
                        pltpu.VMEM((4, n_c + n_l, DK + chunk, DV), BF16),
                        pltpu.VMEM((4, n_c + n_l, DK, DV), F32),
                        pltpu.VMEM((4, n_c + n_l, 8, LANES), F32),
                        pltpu.VMEM((4, DK, DV), F32),
                        pltpu.VMEM((seq_l, 2 * DV), F32)],
        compiler_params=_cparams(("parallel", "parallel"), 56),
        name="dn_delta",
    )(qc, kc, vc, grc, ql, kl, vl, grl, p_lat, norm_g.reshape(1, DV))


def _chan_dft_kernel(tab_ref, w_ref, o_ref):
    o_ref[...] = _dot(tab_ref[0], w_ref[...]).astype(o_ref.dtype)


def _chan_dft_weights(tabs, w):
    d = w.shape[0]
    g = tabs.shape[1]
    tn = 512
    nj = d // tn
    return pl.pallas_call(
        _chan_dft_kernel,
        out_shape=jax.ShapeDtypeStruct((d, 2 * d), BF16),
        grid=(d // g, 2, nj),
        in_specs=[pl.BlockSpec((1, g, g), lambda i, t, j: (t, 0, 0)),
                  pl.BlockSpec((g, tn), lambda i, t, j: (i, j))],
        out_specs=pl.BlockSpec((g, tn), lambda i, t, j: (i, t * nj + j)),
        compiler_params=_cparams(("parallel", "parallel", "parallel"), 32),
        name="fourier_chan",
    )(tabs, w)


def _dft_fold_kernel(lo_ref, hi_ref, ev_ref, od_ref, *, n, tr, d):
    lo = lo_ref[0].astype(F32)
    hi = hi_ref[0].astype(F32)
    ev_ref[0] = (lo + hi).astype(ev_ref.dtype)
    diff = lo - hi
    ad, bd = diff[:, :d], diff[:, d:]
    pos = pl.program_id(1) * tr + lax.broadcasted_iota(jnp.int32, (tr, 1), 0)
    theta = pos.astype(F32) * (2.0 * math.pi / n)
    c, s = jnp.cos(theta), jnp.sin(theta)
    od_ref[0, :, :d] = (ad * c - bd * s).astype(od_ref.dtype)
    od_ref[0, :, d:] = (bd * c + ad * s).astype(od_ref.dtype)


def _dft_fold(ab):
    b, n, d2 = ab.shape
    half = n // 2
    tr = min(256, half)
    nb = half // tr
    out = jax.ShapeDtypeStruct((b, half, d2), ab.dtype)
    spec = pl.BlockSpec((1, tr, d2), lambda i, t: (i, t, 0))
    return pl.pallas_call(
        functools.partial(_dft_fold_kernel, n=n, tr=tr, d=d2 // 2),
        out_shape=(out, out),
        grid=(b, nb),
        in_specs=[spec, pl.BlockSpec((1, tr, d2), lambda i, t: (i, t + nb, 0))],
        out_specs=(spec, spec),
        compiler_params=_cparams(("parallel", "parallel"), 40),
        name="fourier_fold",
    )(ab, ab)


def _seq_dft_kernel(cos_ref, sin_ref, ae_ref, be_ref, ao_ref, bo_ref, mod_ref, r_ref, o_ref,
                    mix_ref, *, scale, tm):
    cos_t, sin_t = cos_ref[...], sin_ref[...]
    gate = mod_ref[0, 2:3, :] * scale
    even = gate * (_dot(cos_t, ae_ref[0]) - _dot(sin_t, be_ref[0]))
    odd = gate * (_dot(cos_t, ao_ref[0]) - _dot(sin_t, bo_ref[0]))
    for q in range(mix_ref.shape[0]):
        lanes = slice(q * LANES, (q + 1) * LANES)
        mix_ref[q, pl.ds(0, tm, stride=2), :] = even[:, lanes]
        mix_ref[q, pl.ds(1, tm, stride=2), :] = odd[:, lanes]
        o_ref[0, :, lanes] = r_ref[0, :, lanes] + mix_ref[q]


def _seq_dft(cos_t, sin_t, ab_even, ab_odd, modv, resid, scale):
    b, half, d2 = ab_even.shape
    d = d2 // 2
    tm, tn = min(512, half), 512
    nj = d // tn
    tab = pl.BlockSpec((tm, half), lambda n, i, j: (i, 0))
    a_spec = pl.BlockSpec((1, half, tn), lambda n, i, j: (n, 0, j))
    b_spec = pl.BlockSpec((1, half, tn), lambda n, i, j: (n, 0, j + nj))
    io = pl.BlockSpec((1, 2 * tm, tn), lambda n, i, j: (n, i, j))
    return pl.pallas_call(
        functools.partial(_seq_dft_kernel, scale=scale, tm=tm),
        out_shape=jax.ShapeDtypeStruct((b, 2 * half, d), F32),
        grid=(b, half // tm, nj),
        in_specs=[tab, tab, a_spec, b_spec, a_spec, b_spec,
                  pl.BlockSpec((1, 8, tn), lambda n, i, j: (n, 0, j)), io],
        out_specs=io,
        scratch_shapes=[pltpu.VMEM((tn // LANES, 2 * tm, LANES), F32)],
        compiler_params=_cparams(("parallel", "parallel", "arbitrary"), 48),
        name="fourier_seq",
    )(cos_t, sin_t, ab_even, ab_even, ab_odd, ab_odd, modv, resid)


def _dft_table_kernel(cos_ref, sin_ref, tc_ref, ts_ref, *, n, tk):
    w = 2.0 * math.pi / n

    def unit(phase):
        ang = (phase & (n - 1)).astype(F32) * w
        return jnp.cos(ang), jnp.sin(ang)

    @pl.when(pl.program_id(0) == 0)
    def _():
        a = lax.broadcasted_iota(jnp.int32, (tk, n), 0)
        m = lax.broadcasted_iota(jnp.int32, (tk, n), 1)
        tc_ref[...], ts_ref[...] = unit(a * m)

    m1 = lax.broadcasted_iota(jnp.int32, (1, n), 1)
    rc, rs = unit((pl.program_id(0) * tk) * m1)
    tcv, tsv = tc_ref[...], ts_ref[...]
    cos_ref[...] = (rc * tcv - rs * tsv).astype(cos_ref.dtype)
    sin_ref[...] = (rs * tcv + rc * tsv).astype(sin_ref.dtype)


def _dft_tables(n):
    tk = min(256, n)
    out = jax.ShapeDtypeStruct((n, n), BF16)
    spec = pl.BlockSpec((tk, n), lambda i: (i, 0))
    return pl.pallas_call(
        functools.partial(_dft_table_kernel, n=n, tk=tk),
        out_shape=(out, out),
        grid=(n // tk,),
        in_specs=[],
        out_specs=(spec, spec),
        scratch_shapes=[pltpu.VMEM((tk, n), F32), pltpu.VMEM((tk, n), F32)],
        compiler_params=_cparams(("arbitrary",), 40),
        name="dft_tables",
    )()


def _gate_rows(p_ba, chunk):
    b, s, _ = p_ba.shape
    t = p_ba.reshape(b, s, 2, 2, HK, 2)
    return jnp.transpose(t, (0, 4, 2, 3, 5, 1)).reshape(b, 4 * HV, s)


def _gate_params(a_log, dt_bias):
    def rows(v, kind_fill):
        t = jnp.stack([jnp.full_like(v, kind_fill), v]).reshape(2, 2, HK, 2)
        return jnp.transpose(t, (2, 0, 1, 3)).reshape(4 * HV, 1)
    kind = rows(jnp.ones((2, HV), F32), 0.0)
    dirn = rows(jnp.stack([jnp.zeros((HV,), F32), jnp.ones((HV,), F32)]), 0.0)
    return rows(a_log.astype(F32), 0.0), rows(dt_bias.astype(F32), 0.0), kind, dirn


def _chunk_major(rows_t, chunk):
    b, _, s = rows_t.shape
    t = rows_t.reshape(b, HK, 8, s // chunk, chunk)
    return jnp.transpose(t, (0, 1, 3, 2, 4))


def kernel(x, c, ctx, c_ctx, norm_g, mod_w, mod_b, ffn_w_gate, ffn_w_up, ffn_w_down, dn_w_in,
           dn_conv_w, dn_a_log, dn_dt_bias, dn_norm_g, dn_w_out, fn_w_out, final_norm_g):
    bsz, seq, d = x.shape
    seq_c = ctx.shape[1]
    t_lat, t_ctx = bsz * seq, bsz * seq_c
    key_dim, val_dim = HK * DK, HV * DV
    qkv_dim = 2 * key_dim + val_dim
    chunk = DELTA_CHUNK

    cond = jnp.zeros((8, d), F32).at[:bsz].set(c).at[bsz].set(c_ctx)
    mod = _modulation(cond, mod_w, mod_b)

    def modv(layer, sub, ctx_rows=False):
        m3 = mod[layer, :, 3 * sub * d:3 * (sub + 1) * d].reshape(8, 3, d)
        m3 = m3[bsz:bsz + 1] if ctx_rows else m3[:bsz]
        return jnp.pad(m3, ((0, 0), (0, 5), (0, 0)))

    w_ffn = (ffn_w_gate, ffn_w_up, ffn_w_down)

    h = x.reshape(t_lat, d)
    hc = ctx.reshape(t_ctx, d)

    h = _ffn(h, modv(0, 0), norm_g[0, 0], *w_ffn, 0, 0)
    hc = _ffn(hc, modv(0, 0, True), norm_g[0, 0], *w_ffn, 0, 0)

    n_in = dn_w_in.shape[2]
    w_in = dn_w_in[0].astype(BF16)
    p = _matmul(h, w_in, out_dtype=F32, tm=2048, tn=512, modv=modv(0, 1), norm_g=norm_g[0, 1],
                lhs_buffers=1, name="dn_in_proj").reshape(bsz, seq, n_in)
    pc = _matmul(hc, w_in, out_dtype=F32, tm=512, tn=512, modv=modv(0, 1, True),
                 norm_g=norm_g[0, 1], name="dn_in_proj_ctx").reshape(bsz, seq_c, n_in)

    conv_w = dn_conv_w[0]
    a_col, dt_col, kind_col, dir_col = _gate_params(dn_a_log[0], dn_dt_bias[0])

    def dn_inputs(pp):
        q = _conv_act(pp, conv_w, 0, key_dim, l2norm=True, out_scale=DK ** -0.5)
        k = _conv_act(pp, conv_w, key_dim, key_dim, l2norm=True)
        v = _conv_act(pp, conv_w, 2 * key_dim, val_dim, l2norm=False)
        ba = pp[:, :, qkv_dim + val_dim:qkv_dim + val_dim + 4 * HV]
        gr = _gates(_gate_rows(ba, chunk), a_col, dt_col, kind_col, dir_col, chunk)
        return q, k, v, _chunk_major(gr, chunk)

    ql, kl, vl, grl = dn_inputs(p)
    qc, kc, vc, grc = dn_inputs(pc)
    y = _delta(qc, kc, vc, grc, ql, kl, vl, grl, p, qkv_dim, dn_norm_g[0], chunk)
    h = _matmul(y.reshape(t_lat, val_dim), dn_w_out[0].astype(BF16), out_dtype=F32, tm=1024,
                tn=512, modv=modv(0, 1), resid=h, name="dn_out_proj")
    h = _ffn(h, modv(0, 2), norm_g[0, 2], *w_ffn, 0, 1)

    h = _ffn(h, modv(1, 0), norm_g[1, 0], *w_ffn, 1, 0)
    gdim = d // N_FGROUPS
    cos_c, sin_c = _dft_tables(gdim)
    w_cs = _chan_dft_weights(jnp.stack([cos_c, sin_c]), fn_w_out[0].astype(BF16))
    ab = _matmul(h, w_cs, out_dtype=BF16, tm=1024, tn=1024, modv=modv(1, 1), norm_g=norm_g[1, 1],
                 name="fourier_in").reshape(bsz, seq, 2 * d)
    cos_l, sin_l = _dft_tables(seq // 2)
    ab_even, ab_odd = _dft_fold(ab)
    h = _seq_dft(cos_l, sin_l, ab_even, ab_odd, modv(1, 1), h.reshape(bsz, seq, d),
                 1.0 / math.sqrt(seq * gdim)).reshape(t_lat, d)
    h = _ffn(h, modv(1, 2), norm_g[1, 2], *w_ffn, 1, 1, final_g=final_norm_g)
    return h.reshape(bsz, seq, d)
```

```python
import functools
import math

import jax
import jax.numpy as jnp
from jax import lax
from jax.experimental import pallas as pl
from jax.experimental.pallas import tpu as pltpu

F32 = jnp.float32
BF16 = jnp.bfloat16

EPS = 1e-6
FFN_RES = 0.5
N_SUB = 3
HK = 16
HV = 32
DK = 128
DV = 128
CONV_W = 5
N_FGROUPS = 4
DELTA_CHUNK = 128
PHASE_A_GROUP = 4

LANES = 128
MIB = 1024 * 1024


def _cparams(semantics, vmem_mib):
    return pltpu.CompilerParams(dimension_semantics=semantics,
                                vmem_limit_bytes=int(vmem_mib * MIB))


def _dot(a, b):
    return jnp.dot(a, b, preferred_element_type=F32)


def _dot_nt(a, b):
    return lax.dot_general(a, b, (((1,), (1,)), ((), ())), preferred_element_type=F32)


def _dot_tn(a, b):
    return lax.dot_general(a, b, (((0,), (0,)), ((), ())), preferred_element_type=F32)


def _silu(x):
    return x * jax.nn.sigmoid(x)


_NORM_ROWS = 32


def _ada_norm_into(h_ref, x_ref, g_ref, mod_ref):
    gain = g_ref[...] * (1.0 + mod_ref[0, 1:2, :])
    shift = mod_ref[0, 0:1, :]
    rows = min(_NORM_ROWS, x_ref.shape[0])

    def step(t, carry):
        r0 = pl.multiple_of(t * rows, rows)
        x = x_ref[pl.ds(r0, rows), :]
        ms = jnp.mean(x * x, axis=-1, keepdims=True)
        h_ref[pl.ds(r0, rows), :] = (x * lax.rsqrt(ms + EPS) * gain + shift).astype(BF16)
        return carry

    lax.fori_loop(0, x_ref.shape[0] // rows, step, 0, unroll=2)


def _mod_kernel(cond_ref, w_ref, b_ref, o_ref):
    s = _silu(cond_ref[...]).astype(BF16)
    o_ref[0] = _dot(s, w_ref[0].astype(BF16)) + b_ref[0]


def _modulation(cond, mod_w, mod_b):
    depth, d, n = mod_w.shape
    tn = 1024
    return pl.pallas_call(
        _mod_kernel,
        out_shape=jax.ShapeDtypeStruct((depth, 8, n), F32),
        grid=(depth, n // tn),
        in_specs=[pl.BlockSpec((8, d), lambda l, j: (0, 0)),
                  pl.BlockSpec((1, d, tn), lambda l, j: (l, 0, j)),
                  pl.BlockSpec((1, 1, tn), lambda l, j: (l, 0, j))],
        out_specs=pl.BlockSpec((1, 8, tn), lambda l, j: (l, 0, j)),
        compiler_params=_cparams(("parallel", "parallel"), 40),
        name="modulation",
    )(cond, mod_w, mod_b.reshape(depth, 1, n))


def _ffn_kernel(x_ref, mod_ref, g_ref, wg_ref, wu_ref, wd_ref, fg_ref, o_ref, h_ref, *,
                final_norm):
    j = pl.program_id(1)
    nj = pl.num_programs(1)

    @pl.when(j == 0)
    def _():
        _ada_norm_into(h_ref, x_ref, g_ref, mod_ref)
        o_ref[...] = jnp.zeros_like(o_ref)

    h = h_ref[...]
    a = (_silu(_dot(h, wg_ref[...].astype(BF16)))
         * _dot(h, wu_ref[...].astype(BF16))).astype(BF16)
    o_ref[...] += _dot(a, wd_ref[...].astype(BF16))

    @pl.when(j == nj - 1)
    def _():
        gate = FFN_RES * mod_ref[0, 2:3, :]
        rows = min(_NORM_ROWS, x_ref.shape[0])

        def step(t, carry):
            r0 = pl.multiple_of(t * rows, rows)
            r = x_ref[pl.ds(r0, rows), :] + gate * o_ref[pl.ds(r0, rows), :]
            if final_norm:
                ms = jnp.mean(r * r, axis=-1, keepdims=True)
                r = r * lax.rsqrt(ms + EPS) * fg_ref[...]
            o_ref[pl.ds(r0, rows), :] = r
            return carry

        lax.fori_loop(0, x_ref.shape[0] // rows, step, 0, unroll=2)


def _ffn(x, modv, g, wg, wu, wd, layer, which, final_g=None):
    t, d = x.shape
    f = wg.shape[3]
    tm = min(1024, t // modv.shape[0])
    tf = 256
    tiles_per_mod = t // tm // modv.shape[0]
    fg = (final_g if final_g is not None else g).reshape(1, d)
    return pl.pallas_call(
        functools.partial(_ffn_kernel, final_norm=final_g is not None),
        out_shape=jax.ShapeDtypeStruct((t, d), F32),
        grid=(t // tm, f // tf),
        in_specs=[pl.BlockSpec((tm, d), lambda i, j: (i, 0)),
                  pl.BlockSpec((1, 8, d), lambda i, j: (i // tiles_per_mod, 0, 0)),
                  pl.BlockSpec((1, d), lambda i, j: (0, 0)),
                  pl.BlockSpec((None, None, d, tf), lambda i, j: (layer, which, 0, j)),
                  pl.BlockSpec((None, None, d, tf), lambda i, j: (layer, which, 0, j)),
                  pl.BlockSpec((None, None, tf, d), lambda i, j: (layer, which, j, 0)),
                  pl.BlockSpec((1, d), lambda i, j: (0, 0))],
        out_specs=pl.BlockSpec((tm, d), lambda i, j: (i, 0)),
        scratch_shapes=[pltpu.VMEM((tm, d), BF16)],
        compiler_params=_cparams(("parallel", "arbitrary"), 56),
        name="ffn_half",
    )(x, modv, g.reshape(1, d), wg, wu, wd, fg)


def _mm_kernel(*refs, norm, resid):
    it = iter(refs)
    a_ref = next(it)
    b_ref = next(it)
    if norm:
        mod_ref, g_ref = next(it), next(it)
    elif resid:
        mod_ref = next(it)
    if resid:
        r_ref = next(it)
    o_ref = next(it)

    if norm:
        h_ref = next(it)

        @pl.when(pl.program_id(1) == 0)
        def _():
            _ada_norm_into(h_ref, a_ref, g_ref, mod_ref)
        a = h_ref[...]
    else:
        a = a_ref[...]
    acc = _dot(a, b_ref[...])
    if resid:
        acc = r_ref[...] + mod_ref[0, 2:3, :] * acc
    o_ref[...] = acc.astype(o_ref.dtype)


def _matmul(a, b, *, out_dtype, tm, tn, modv=None, norm_g=None, resid=None, vmem_mib=48,
            lhs_buffers=2, name="matmul"):
    m, k = a.shape
    n = b.shape[1]
    tm = min(tm, m if modv is None else m // modv.shape[0])
    norm = norm_g is not None
    has_res = resid is not None
    lhs_mode = {} if lhs_buffers == 2 else {"pipeline_mode": pl.Buffered(lhs_buffers)}
    in_specs = [pl.BlockSpec((tm, k), lambda i, j: (i, 0), **lhs_mode),
                pl.BlockSpec((k, tn), lambda i, j: (0, j))]
    args = [a, b]
    if norm or has_res:
        tiles_per_mod = m // tm // modv.shape[0]
        if norm:
            in_specs.append(pl.BlockSpec((1, 8, k), lambda i, j: (i // tiles_per_mod, 0, 0)))
            in_specs.append(pl.BlockSpec((1, k), lambda i, j: (0, 0)))
            args += [modv, norm_g.reshape(1, k)]
        else:
            in_specs.append(pl.BlockSpec((1, 8, tn), lambda i, j: (i // tiles_per_mod, 0, j)))
            args.append(modv)
    if has_res:
        in_specs.append(pl.BlockSpec((tm, tn), lambda i, j: (i, j)))
        args.append(resid)
    return pl.pallas_call(
        functools.partial(_mm_kernel, norm=norm, resid=has_res),
        out_shape=jax.ShapeDtypeStruct((m, n), out_dtype),
        grid=(m // tm, pl.cdiv(n, tn)),
        in_specs=in_specs,
        out_specs=pl.BlockSpec((tm, tn), lambda i, j: (i, j)),
        scratch_shapes=[pltpu.VMEM((tm, k), BF16)] if norm else [],
        compiler_params=_cparams(("parallel", "arbitrary"), vmem_mib),
        name=name,
    )(*args)


_CONV_ROWS = 256
_CONV_HALO = 8


def _conv_kernel(x_ref, w_ref, o_ref, xp_ref, *, seq, tc, l2norm, out_scale):
    zeros = jnp.zeros((_CONV_HALO, tc), F32)
    xp_ref[0:_CONV_HALO, :] = zeros
    xp_ref[_CONV_HALO + seq:, :] = zeros
    xp_ref[_CONV_HALO:_CONV_HALO + seq, :] = x_ref[0]
    w = w_ref[...]
    rows = min(_CONV_ROWS, seq)

    def step(t):
        base = t * rows
        acc = None
        for j in range(CONV_W):
            start = base + _CONV_HALO - CONV_W // 2 + j
            term = xp_ref[start:start + rows, :] * w[j:j + 1, :]
            acc = term if acc is None else acc + term
        y = _silu(acc)
        if l2norm:
            parts = []
            for hh in range(tc // LANES):
                yh = y[:, hh * LANES:(hh + 1) * LANES]
                ss = jnp.sum(yh * yh, axis=-1, keepdims=True)
                parts.append(yh * (lax.rsqrt(ss + EPS) * out_scale))
            y = parts[0] if len(parts) == 1 else jnp.concatenate(parts, axis=-1)
        o_ref[0, base:base + rows, :] = y.astype(o_ref.dtype)

    for t in range(seq // rows):
        step(t)


def _conv_act(p, conv_w, col0, ncols, *, l2norm, out_scale=1.0):
    b, s, _ = p.shape
    tc = 256
    off = col0 // tc
    return pl.pallas_call(
        functools.partial(_conv_kernel, seq=s, tc=tc, l2norm=l2norm, out_scale=out_scale),
        out_shape=jax.ShapeDtypeStruct((b, s, ncols), BF16),
        grid=(b, ncols // tc),
        in_specs=[pl.BlockSpec((1, s, tc), lambda i, j: (i, 0, j + off)),
                  pl.BlockSpec((CONV_W, tc), lambda i, j: (0, j + off))],
        out_specs=pl.BlockSpec((1, s, tc), lambda i, j: (i, 0, j)),
        scratch_shapes=[pltpu.VMEM((s + 2 * _CONV_HALO, tc), F32)],
        compiler_params=_cparams(("parallel", "parallel"), 40),
        name="dn_conv",
    )(p, conv_w)


def _split3(x):
    x1 = x.astype(BF16)
    r1 = x - x1.astype(F32)
    x2 = r1.astype(BF16)
    x3 = (r1 - x2.astype(F32)).astype(BF16)
    return x1, x2, x3


def _gates_kernel(ba_ref, a_ref, dt_ref, kind_ref, dir_ref, o_ref, *, chunk, tl):
    x = ba_ref[0]
    beta = jax.nn.sigmoid(x)
    y = x + dt_ref[...]
    softplus = jnp.maximum(y, 0.0) + jnp.log(1.0 + jnp.exp(-jnp.abs(y)))
    g = -jnp.exp(a_ref[...]) * softplus
    i = lax.broadcasted_iota(jnp.int32, (tl, tl), 0)
    j = lax.broadcasted_iota(jnp.int32, (tl, tl), 1)
    sh = int(math.log2(chunk))
    same = lax.shift_right_logical(i, sh) == lax.shift_right_logical(j, sh)
    pre = jnp.where(same & (i <= j), 1.0, 0.0).astype(BF16)
    suf = jnp.where(same & (i >= j), 1.0, 0.0).astype(BF16)
    g1, g2, g3 = _split3(g)
    cpre = _dot(g1, pre) + _dot(g2, pre) + _dot(g3, pre)
    csuf = _dot(g1, suf) + _dot(g2, suf) + _dot(g3, suf)
    cum = jnp.where(dir_ref[...] > 0.5, csuf, cpre)
    o_ref[0] = jnp.where(kind_ref[...] > 0.5, cum, beta)


def _gates(ba_t, a_col, dt_col, kind_col, dir_col, chunk):
    b, r, s = ba_t.shape
    tl = min(512, s)
    col = pl.BlockSpec((r, 1), lambda i, j: (0, 0))
    return pl.pallas_call(
        functools.partial(_gates_kernel, chunk=chunk, tl=tl),
        out_shape=jax.ShapeDtypeStruct((b, r, s), F32),
        grid=(b, s // tl),
        in_specs=[pl.BlockSpec((1, r, tl), lambda i, j: (i, 0, j)), col, col, col, col],
        out_specs=pl.BlockSpec((1, r, tl), lambda i, j: (i, 0, j)),
        compiler_params=_cparams(("parallel", "parallel"), 32),
        name="dn_gates",
    )(ba_t, a_col, dt_col, kind_col, dir_col)


def _col_form(gr_ref, dst_ref, n_rows, chunk):
    pad = jnp.zeros((LANES - 8, LANES), F32)
    for s in range(n_rows // LANES):
        if chunk <= LANES:
            per = LANES // chunk
            parts = [gr_ref[0, 0, s * per + t] for t in range(per)]
            slab = parts[0] if per == 1 else jnp.concatenate(parts, axis=-1)
        else:
            per = chunk // LANES
            slab = gr_ref[0, 0, s // per, :, (s % per) * LANES:(s % per + 1) * LANES]
        dst_ref[s * LANES:(s + 1) * LANES, :] = jnp.concatenate([slab, pad], axis=0).T


def _delta_kernel(qc_ref, kc_ref, vc_ref, grc_ref, ql_ref, kl_ref, vl_ref, grl_ref,
                  z_ref, ng_ref, o_ref, colc_ref, coll_ref, aq_ref, b_ref, gl_ref, s_ref,
                  oacc_ref, *, chunk, seq_c, seq_l):
    C = chunk
    n_c, n_l = seq_c // C, seq_l // C
    _col_form(grc_ref, colc_ref, seq_c, C)
    _col_form(grl_ref, coll_ref, seq_l, C)
    s_ref[...] = jnp.zeros_like(s_ref)
    oacc_ref[...] = jnp.zeros_like(oacc_ref)

    row = lax.broadcasted_iota(jnp.int32, (C, C), 0)
    col = lax.broadcasted_iota(jnp.int32, (C, C), 1)
    eye = jnp.where(row == col, 1.0, 0.0)

    def same_block(n):
        return lax.shift_right_logical(row, n) == lax.shift_right_logical(col, n)

    pair_mask = jnp.where(same_block(1), 1.0, 0.0)
    join_masks = [jnp.where(same_block(n + 1), 1.0, 0.0) - jnp.where(same_block(n), 1.0, 0.0)
                  for n in range(1, int(math.log2(C)))]

    chains = range(4)

    def take_rows(a, blk, d):
        off = blk if d == 0 else 0
        return jnp.concatenate([a[s + off:s + off + blk] for s in range(0, C, 2 * blk)], axis=0)

    def put_rows(a, new, blk, d):
        parts = []
        for i, s in enumerate(range(0, C, 2 * blk)):
            upd = new[i * blk:(i + 1) * blk]
            parts += [a[s:s + blk], upd] if d == 0 else [upd, a[s + blk:s + 2 * blk]]
        return jnp.concatenate(parts, axis=0)

    def phase_a(q_ref, k_ref, v_ref, gr_ref, col_ref, slot0, want_out, group):
        items = [(g, c) for g in range(group) for c in chains]

        def body(it, carry):
            cis = [it * group + g for g in range(group)]
            r0s = [pl.multiple_of(ci * C, C) for ci in cis]
            kb = [k_ref[0, pl.ds(r0, C), :] for r0 in r0s]
            kf = [t.astype(F32) for t in kb]
            gram = [_dot_nt(t, t) for t in kb]
            if want_out:
                qb = [q_ref[0, pl.ds(r0, C), :] for r0 in r0s]
                qk = [_dot_nt(qb[g], kb[g]) for g in range(group)]
            cols = [col_ref[pl.ds(r0, C), :] for r0 in r0s]
            beta, gcol, decay, lm, x = [], [], [], [], []
            for n, (g, c) in enumerate(items):
                d = c // 2
                beta.append(cols[g][:, c:c + 1])
                gcol.append(cols[g][:, 4 + c:5 + c])
                grow = gr_ref[0, 0, cis[g], 4 + c:5 + c, :]
                incl = (row >= col) if d == 0 else (row <= col)
                strict = (row > col) if d == 0 else (row < col)
                decay.append(jnp.exp(jnp.where(incl, gcol[n] - grow, -jnp.inf)))
                lm.append(jnp.where(strict, gram[g] * decay[n] * beta[n], 0.0))
                x.append(eye - lm[n] * pair_mask)
            idx = range(len(items))
            for lvl, jm in enumerate(join_masks, start=1):
                blk = 1 << lvl
                if blk < 8:
                    y = [_dot((lm[n] * jm).astype(BF16), x[n].astype(BF16)) for n in idx]
                    x = [x[n] - _dot(x[n].astype(BF16), y[n].astype(BF16)) for n in idx]
                    continue
                jmh = [take_rows(jm, blk, d) for d in range(2)]
                zeros = jnp.zeros((blk, C), F32)
                y = [_dot((take_rows(lm[n], blk, items[n][1] // 2)
                           * jmh[items[n][1] // 2]).astype(BF16), x[n].astype(BF16)) for n in idx]
                new = []
                for n in idx:
                    d = items[n][1] // 2
                    yfull = jnp.concatenate(
                        [t for s in range(C // (2 * blk))
                         for t in ((zeros, y[n][s * blk:(s + 1) * blk]) if d == 0
                                   else (y[n][s * blk:(s + 1) * blk], zeros))], axis=0)
                    xa = take_rows(x[n], blk, d)
                    new.append(xa - _dot(xa.astype(BF16), yfull.astype(BF16)))
                x = [put_rows(x[n], new[n], blk, items[n][1] // 2) for n in idx]
            eg = [jnp.exp(gcol[n]) for n in idx]
            sol = []
            for n, (g, c) in enumerate(items):
                e = c % 2
                vf = v_ref[0, pl.ds(r0s[g], C), e * DV:(e + 1) * DV].astype(F32)
                rhs = jnp.concatenate([vf * beta[n], kf[g] * (beta[n] * eg[n])], axis=-1)
                sol.append(_dot(x[n].astype(BF16), rhs.astype(BF16)).astype(BF16))
            kw = []
            for n, (g, c) in enumerate(items):
                glast = gcol[n][C - 1:C, :] if c // 2 == 0 else gcol[n][0:1, :]
                ktail = (kf[g] * jnp.exp(glast - gcol[n])).astype(BF16)
                kw.append(_dot_tn(ktail, sol[n]))
                gl_ref[c, slot0 + cis[g]] = jnp.broadcast_to(jnp.exp(glast), (8, LANES))
            if want_out:
                aw = [_dot((qk[g] * decay[n]).astype(BF16), sol[n])
                      for n, (g, c) in enumerate(items)]
            for n, (g, c) in enumerate(items):
                slot = slot0 + cis[g]
                b_ref[c, slot] = kw[n][:, :DV]
                aq_ref[c, slot, 0:DK, :] = (-kw[n][:, DV:]).astype(BF16)
                if want_out:
                    aq_ref[c, slot, DK:DK + C, :] = (qb[g].astype(F32) * eg[n]
                                                     - aw[n][:, DV:]).astype(BF16)
            if want_out:
                for g in range(group):
                    for e in range(2):
                        oacc_ref[pl.ds(r0s[g], C), e * DV:(e + 1) * DV] += (
                            aw[4 * g + e][:, :DV] + aw[4 * g + 2 + e][:, :DV])
            return carry
        return body

    def phase_b(n, slot0, want_out):
        def body(i, carry):
            states = [s_ref[c] for c in chains]
            cis = [i if c // 2 == 0 else n - 1 - i for c in chains]
            res = []
            for c in chains:
                slot = slot0 + cis[c]
                lhs = aq_ref[c, slot] if want_out else aq_ref[c, slot, 0:DK, :]
                res.append(_dot(lhs, states[c].astype(BF16)))
            for c in chains:
                slot = slot0 + cis[c]
                s_ref[c] = states[c] * gl_ref[c, slot, 0:1, :] + res[c][:DK] + b_ref[c, slot]
                if want_out:
                    r0 = pl.multiple_of(cis[c] * C, C)
                    e = c % 2
                    oacc_ref[pl.ds(r0, C), e * DV:(e + 1) * DV] += res[c][DK:]
            return carry
        return body

    grp_c = math.gcd(n_c, PHASE_A_GROUP)
    grp_l = math.gcd(n_l, PHASE_A_GROUP)
    lax.fori_loop(0, n_c // grp_c,
                  phase_a(qc_ref, kc_ref, vc_ref, grc_ref, colc_ref, 0, False, grp_c), 0)
    lax.fori_loop(0, n_l // grp_l,
                  phase_a(ql_ref, kl_ref, vl_ref, grl_ref, coll_ref, n_c, True, grp_l), 0)
    lax.fori_loop(0, n_c, phase_b(n_c, 0, False), 0)
    lax.fori_loop(0, n_l, phase_b(n_l, n_c, True), 0, unroll=4)

    rows = min(512, seq_l)

    def epilogue(t, carry):
        base = pl.multiple_of(t * rows, 8)
        for e in range(2):
            o = oacc_ref[pl.ds(base, rows), e * DV:(e + 1) * DV]
            ms = jnp.mean(o * o, axis=-1, keepdims=True)
            y = o * lax.rsqrt(ms + EPS) * ng_ref[...]
            zz = z_ref[0, pl.ds(base, rows), e * DV:(e + 1) * DV]
            o_ref[0, pl.ds(base, rows), e * DV:(e + 1) * DV] = (y * _silu(zz)).astype(o_ref.dtype)
        return carry

    lax.fori_loop(0, seq_l // rows, epilogue, 0)


def _delta(qc, kc, vc, grc, ql, kl, vl, grl, p_lat, z_col0, norm_g, chunk):
    b, seq_c, _ = qc.shape
    seq_l = ql.shape[1]
    n_c, n_l = seq_c // chunk, seq_l // chunk
    zoff = z_col0 // (2 * DV)

    def tok(seq, width):
        return pl.BlockSpec((1, seq, width), lambda i, h: (i, 0, h))

    def gate_spec(n):
        return pl.BlockSpec((1, 1, n, 8, chunk), lambda i, h: (i, h, 0, 0, 0))

    return pl.pallas_call(
        functools.partial(_delta_kernel, chunk=chunk, seq_c=seq_c, seq_l=seq_l),
        out_shape=jax.ShapeDtypeStruct((b, seq_l, HV * DV), BF16),
        grid=(b, HK),
        in_specs=[tok(seq_c, DK), tok(seq_c, DK), tok(seq_c, 2 * DV), gate_spec(n_c),
                  tok(seq_l, DK), tok(seq_l, DK), tok(seq_l, 2 * DV), gate_spec(n_l),
                  pl.BlockSpec((1, seq_l, 2 * DV), lambda i, h: (i, 0, h + zoff)),
                  pl.BlockSpec((1, DV), lambda i, h: (0, 0))],
        out_specs=tok(seq_l, 2 * DV),
        scratch_shapes=[pltpu.VMEM((seq_c, LANES), F32),
                        pltpu.VMEM((seq_l, LANES), F32),
                        pltpu.VMEM((4, n_c + n_l, DK + chunk, DV), BF16),
                        pltpu.VMEM((4, n_c + n_l, DK, DV), F32),
                        pltpu.VMEM((4, n_c + n_l, 8, LANES), F32),
                        pltpu.VMEM((4, DK, DV), F32),
                        pltpu.VMEM((seq_l, 2 * DV), F32)],
        compiler_params=_cparams(("parallel", "parallel"), 56),
        name="dn_delta",
    )(qc, kc, vc, grc, ql, kl, vl, grl, p_lat, norm_g.reshape(1, DV))


def _chan_dft_kernel(tab_ref, w_ref, o_ref):
    o_ref[...] = _dot(tab_ref[0], w_ref[...]).astype(o_ref.dtype)


def _chan_dft_weights(tabs, w):
    d = w.shape[0]
    g = tabs.shape[1]
    tn = 512
    nj = d // tn
    return pl.pallas_call(
        _chan_dft_kernel,
        out_shape=jax.ShapeDtypeStruct((d, 2 * d), BF16),
        grid=(d // g, 2, nj),
        in_specs=[pl.BlockSpec((1, g, g), lambda i, t, j: (t, 0, 0)),
                  pl.BlockSpec((g, tn), lambda i, t, j: (i, j))],
        out_specs=pl.BlockSpec((g, tn), lambda i, t, j: (i, t * nj + j)),
        compiler_params=_cparams(("parallel", "parallel", "parallel"), 32),
        name="fourier_chan",
    )(tabs, w)


def _dft_fold_kernel(lo_ref, hi_ref, ev_ref, od_ref, *, n, tr, d):
    lo = lo_ref[0].astype(F32)
    hi = hi_ref[0].astype(F32)
    ev_ref[0] = (lo + hi).astype(ev_ref.dtype)
    diff = lo - hi
    ad, bd = diff[:, :d], diff[:, d:]
    pos = pl.program_id(1) * tr + lax.broadcasted_iota(jnp.int32, (tr, 1), 0)
    theta = pos.astype(F32) * (2.0 * math.pi / n)
    c, s = jnp.cos(theta), jnp.sin(theta)
    od_ref[0, :, :d] = (ad * c - bd * s).astype(od_ref.dtype)
    od_ref[0, :, d:] = (bd * c + ad * s).astype(od_ref.dtype)


def _dft_fold(ab):
    b, n, d2 = ab.shape
    half = n // 2
    tr = min(256, half)
    nb = half // tr
    out = jax.ShapeDtypeStruct((b, half, d2), ab.dtype)
    spec = pl.BlockSpec((1, tr, d2), lambda i, t: (i, t, 0))
    return pl.pallas_call(
        functools.partial(_dft_fold_kernel, n=n, tr=tr, d=d2 // 2),
        out_shape=(out, out),
        grid=(b, nb),
        in_specs=[spec, pl.BlockSpec((1, tr, d2), lambda i, t: (i, t + nb, 0))],
        out_specs=(spec, spec),
        compiler_params=_cparams(("parallel", "parallel"), 40),
        name="fourier_fold",
    )(ab, ab)


def _seq_dft_kernel(cos_ref, sin_ref, ae_ref, be_ref, ao_ref, bo_ref, mod_ref, r_ref, o_ref,
                    mix_ref, *, scale, tm):
    cos_t, sin_t = cos_ref[...], sin_ref[...]
    gate = mod_ref[0, 2:3, :] * scale
    even = gate * (_dot(cos_t, ae_ref[0]) - _dot(sin_t, be_ref[0]))
    odd = gate * (_dot(cos_t, ao_ref[0]) - _dot(sin_t, bo_ref[0]))
    for q in range(mix_ref.shape[0]):
        lanes = slice(q * LANES, (q + 1) * LANES)
        mix_ref[q, pl.ds(0, tm, stride=2), :] = even[:, lanes]
        mix_ref[q, pl.ds(1, tm, stride=2), :] = odd[:, lanes]
        o_ref[0, :, lanes] = r_ref[0, :, lanes] + mix_ref[q]


def _seq_dft(cos_t, sin_t, ab_even, ab_odd, modv, resid, scale):
    b, half, d2 = ab_even.shape
    d = d2 // 2
    tm, tn = min(512, half), 512
    nj = d // tn
    tab = pl.BlockSpec((tm, half), lambda n, i, j: (i, 0))
    a_spec = pl.BlockSpec((1, half, tn), lambda n, i, j: (n, 0, j))
    b_spec = pl.BlockSpec((1, half, tn), lambda n, i, j: (n, 0, j + nj))
    io = pl.BlockSpec((1, 2 * tm, tn), lambda n, i, j: (n, i, j))
    return pl.pallas_call(
        functools.partial(_seq_dft_kernel, scale=scale, tm=tm),
        out_shape=jax.ShapeDtypeStruct((b, 2 * half, d), F32),
        grid=(b, half // tm, nj),
        in_specs=[tab, tab, a_spec, b_spec, a_spec, b_spec,
                  pl.BlockSpec((1, 8, tn), lambda n, i, j: (n, 0, j)), io],
        out_specs=io,
        scratch_shapes=[pltpu.VMEM((tn // LANES, 2 * tm, LANES), F32)],
        compiler_params=_cparams(("parallel", "parallel", "arbitrary"), 48),
        name="fourier_seq",
    )(cos_t, sin_t, ab_even, ab_even, ab_odd, ab_odd, modv, resid)


def _dft_table_kernel(cos_ref, sin_ref, tc_ref, ts_ref, *, n, tk):
    w = 2.0 * math.pi / n

    def unit(phase):
        ang = (phase & (n - 1)).astype(F32) * w
        return jnp.cos(ang), jnp.sin(ang)

    @pl.when(pl.program_id(0) == 0)
    def _():
        a = lax.broadcasted_iota(jnp.int32, (tk, n), 0)
        m = lax.broadcasted_iota(jnp.int32, (tk, n), 1)
        tc_ref[...], ts_ref[...] = unit(a * m)

    m1 = lax.broadcasted_iota(jnp.int32, (1, n), 1)
    rc, rs = unit((pl.program_id(0) * tk) * m1)
    tcv, tsv = tc_ref[...], ts_ref[...]
    cos_ref[...] = (rc * tcv - rs * tsv).astype(cos_ref.dtype)
    sin_ref[...] = (rs * tcv + rc * tsv).astype(sin_ref.dtype)


def _dft_tables(n):
    tk = min(256, n)
    out = jax.ShapeDtypeStruct((n, n), BF16)
    spec = pl.BlockSpec((tk, n), lambda i: (i, 0))
    return pl.pallas_call(
        functools.partial(_dft_table_kernel, n=n, tk=tk),
        out_shape=(out, out),
        grid=(n // tk,),
        in_specs=[],
        out_specs=(spec, spec),
        scratch_shapes=[pltpu.VMEM((tk, n), F32), pltpu.VMEM((tk, n), F32)],
        compiler_params=_cparams(("arbitrary",), 40),
        name="dft_tables",
    )()


def _gate_rows(p_ba, chunk):
    b, s, _ = p_ba.shape
    t = p_ba.reshape(b, s, 2, 2, HK, 2)
    return jnp.transpose(t, (0, 4, 2, 3, 5, 1)).reshape(b, 4 * HV, s)


def _gate_params(a_log, dt_bias):
    def rows(v, kind_fill):
        t = jnp.stack([jnp.full_like(v, kind_fill), v]).reshape(2, 2, HK, 2)
        return jnp.transpose(t, (2, 0, 1, 3)).reshape(4 * HV, 1)
    kind = rows(jnp.ones((2, HV), F32), 0.0)
    dirn = rows(jnp.stack([jnp.zeros((HV,), F32), jnp.ones((HV,), F32)]), 0.0)
    return rows(a_log.astype(F32), 0.0), rows(dt_bias.astype(F32), 0.0), kind, dirn


def _chunk_major(rows_t, chunk):
    b, _, s = rows_t.shape
    t = rows_t.reshape(b, HK, 8, s // chunk, chunk)
    return jnp.transpose(t, (0, 1, 3, 2, 4))


def kernel(x, c, ctx, c_ctx, norm_g, mod_w, mod_b, ffn_w_gate, ffn_w_up, ffn_w_down, dn_w_in,
           dn_conv_w, dn_a_log, dn_dt_bias, dn_norm_g, dn_w_out, fn_w_out, final_norm_g):
    bsz, seq, d = x.shape
    seq_c = ctx.shape[1]
    t_lat, t_ctx = bsz * seq, bsz * seq_c
    key_dim, val_dim = HK * DK, HV * DV
    qkv_dim = 2 * key_dim + val_dim
    chunk = DELTA_CHUNK

    cond = jnp.zeros((8, d), F32).at[:bsz].set(c).at[bsz].set(c_ctx)
    mod = _modulation(cond, mod_w, mod_b)

    def modv(layer, sub, ctx_rows=False):
        m3 = mod[layer, :, 3 * sub * d:3 * (sub + 1) * d].reshape(8, 3, d)
        m3 = m3[bsz:bsz + 1] if ctx_rows else m3[:bsz]
        return jnp.pad(m3, ((0, 0), (0, 5), (0, 0)))

    w_ffn = (ffn_w_gate, ffn_w_up, ffn_w_down.astype(BF16))

    h = x.reshape(t_lat, d)
    hc = ctx.reshape(t_ctx, d)

    h = _ffn(h, modv(0, 0), norm_g[0, 0], *w_ffn, 0, 0)
    hc = _ffn(hc, modv(0, 0, True), norm_g[0, 0], *w_ffn, 0, 0)

    n_in = dn_w_in.shape[2]
    w_in = dn_w_in[0].astype(BF16)
    p = _matmul(h, w_in, out_dtype=F32, tm=2048, tn=512, modv=modv(0, 1), norm_g=norm_g[0, 1],
                lhs_buffers=1, name="dn_in_proj").reshape(bsz, seq, n_in)
    pc = _matmul(hc, w_in, out_dtype=F32, tm=512, tn=512, modv=modv(0, 1, True),
                 norm_g=norm_g[0, 1], name="dn_in_proj_ctx").reshape(bsz, seq_c, n_in)

    conv_w = dn_conv_w[0]
    a_col, dt_col, kind_col, dir_col = _gate_params(dn_a_log[0], dn_dt_bias[0])

    def dn_inputs(pp):
        q = _conv_act(pp, conv_w, 0, key_dim, l2norm=True, out_scale=DK ** -0.5)
        k = _conv_act(pp, conv_w, key_dim, key_dim, l2norm=True)
        v = _conv_act(pp, conv_w, 2 * key_dim, val_dim, l2norm=False)
        ba = pp[:, :, qkv_dim + val_dim:qkv_dim + val_dim + 4 * HV]
        gr = _gates(_gate_rows(ba, chunk), a_col, dt_col, kind_col, dir_col, chunk)
        return q, k, v, _chunk_major(gr, chunk)

    ql, kl, vl, grl = dn_inputs(p)
    qc, kc, vc, grc = dn_inputs(pc)
    y = _delta(qc, kc, vc, grc, ql, kl, vl, grl, p, qkv_dim, dn_norm_g[0], chunk)
    h = _matmul(y.reshape(t_lat, val_dim), dn_w_out[0].astype(BF16), out_dtype=F32, tm=1024,
                tn=512, modv=modv(0, 1), resid=h, name="dn_out_proj")
    h = _ffn(h, modv(0, 2), norm_g[0, 2], *w_ffn, 0, 1)

    h = _ffn(h, modv(1, 0), norm_g[1, 0], *w_ffn, 1, 0)
    gdim = d // N_FGROUPS
    cos_c, sin_c = _dft_tables(gdim)
    w_cs = _chan_dft_weights(jnp.stack([cos_c, sin_c]), fn_w_out[0].astype(BF16))
    ab = _matmul(h, w_cs, out_dtype=BF16, tm=1024, tn=1024, modv=modv(1, 1), norm_g=norm_g[1, 1],
                 name="fourier_in").reshape(bsz, seq, 2 * d)
    cos_l, sin_l = _dft_tables(seq // 2)
    ab_even, ab_odd = _dft_fold(ab)
    h = _seq_dft(cos_l, sin_l, ab_even, ab_odd, modv(1, 1), h.reshape(bsz, seq, d),
                 1.0 / math.sqrt(seq * gdim)).reshape(t_lat, d)
    h = _ffn(h, modv(1, 2), norm_g[1, 2], *w_ffn, 1, 1, final_g=final_norm_g)
    return h.reshape(bsz, seq, d)
```

```python
import functools
import math

import jax
import jax.numpy as jnp
from jax import lax
from jax.experimental import pallas as pl
from jax.experimental.pallas import tpu as pltpu

F32 = jnp.float32
BF16 = jnp.bfloat16

EPS = 1e-6
FFN_RES = 0.5
N_SUB = 3
HK = 16
HV = 32
DK = 128
DV = 128
CONV_W = 5
N_FGROUPS = 4
DELTA_CHUNK = 128
PHASE_A_GROUP = 4

LANES = 128
MIB = 1024 * 1024


def _cparams(semantics, vmem_mib):
    return pltpu.CompilerParams(dimension_semantics=semantics,
                                vmem_limit_bytes=int(vmem_mib * MIB))


def _dot(a, b):
    return jnp.dot(a, b, preferred_element_type=F32)


def _dot_nt(a, b):
    return lax.dot_general(a, b, (((1,), (1,)), ((), ())), preferred_element_type=F32)


def _dot_tn(a, b):
    return lax.dot_general(a, b, (((0,), (0,)), ((), ())), preferred_element_type=F32)


def _silu(x):
    return x * jax.nn.sigmoid(x)


_NORM_ROWS = 32


def _ada_norm_into(h_ref, x_ref, g_ref, mod_ref):
    gain = g_ref[...] * (1.0 + mod_ref[0, 1:2, :])
    shift = mod_ref[0, 0:1, :]
    rows = min(_NORM_ROWS, x_ref.shape[0])

    def step(t, carry):
        r0 = pl.multiple_of(t * rows, rows)
        x = x_ref[pl.ds(r0, rows), :]
        ms = jnp.mean(x * x, axis=-1, keepdims=True)
        h_ref[pl.ds(r0, rows), :] = (x * lax.rsqrt(ms + EPS) * gain + shift).astype(BF16)
        return carry

    lax.fori_loop(0, x_ref.shape[0] // rows, step, 0, unroll=2)


def _mod_kernel(cond_ref, w_ref, b_ref, o_ref):
    s = _silu(cond_ref[...]).astype(BF16)
    o_ref[0] = _dot(s, w_ref[0].astype(BF16)) + b_ref[0]


def _modulation(cond, mod_w, mod_b):
    depth, d, n = mod_w.shape
    tn = 1024
    return pl.pallas_call(
        _mod_kernel,
        out_shape=jax.ShapeDtypeStruct((depth, 8, n), F32),
        grid=(depth, n // tn),
        in_specs=[pl.BlockSpec((8, d), lambda l, j: (0, 0)),
                  pl.BlockSpec((1, d, tn), lambda l, j: (l, 0, j)),
                  pl.BlockSpec((1, 1, tn), lambda l, j: (l, 0, j))],
        out_specs=pl.BlockSpec((1, 8, tn), lambda l, j: (l, 0, j)),
        compiler_params=_cparams(("parallel", "parallel"), 40),
        name="modulation",
    )(cond, mod_w, mod_b.reshape(depth, 1, n))


def _ffn_kernel(x_ref, mod_ref, g_ref, wg_ref, wu_ref, wd_ref, fg_ref, o_ref, h_ref, *,
                final_norm):
    j = pl.program_id(1)
    nj = pl.num_programs(1)

    @pl.when(j == 0)
    def _():
        _ada_norm_into(h_ref, x_ref, g_ref, mod_ref)
        o_ref[...] = jnp.zeros_like(o_ref)

    h = h_ref[...]
    a = (_silu(_dot(h, wg_ref[...].astype(BF16)))
         * _dot(h, wu_ref[...].astype(BF16))).astype(BF16)
    o_ref[...] += _dot(a, wd_ref[...].astype(BF16))

    @pl.when(j == nj - 1)
    def _():
        gate = FFN_RES * mod_ref[0, 2:3, :]
        rows = min(_NORM_ROWS, x_ref.shape[0])

        def step(t, carry):
            r0 = pl.multiple_of(t * rows, rows)
            r = x_ref[pl.ds(r0, rows), :] + gate * o_ref[pl.ds(r0, rows), :]
            if final_norm:
                ms = jnp.mean(r * r, axis=-1, keepdims=True)
                r = r * lax.rsqrt(ms + EPS) * fg_ref[...]
            o_ref[pl.ds(r0, rows), :] = r
            return carry

        lax.fori_loop(0, x_ref.shape[0] // rows, step, 0, unroll=2)


def _ffn(x, modv, g, wg, wu, wd, layer, which, final_g=None):
    t, d = x.shape
    f = wg.shape[3]
    tm = min(1024, t // modv.shape[0])
    tf = 256
    tiles_per_mod = t // tm // modv.shape[0]
    fg = (final_g if final_g is not None else g).reshape(1, d)
    return pl.pallas_call(
        functools.partial(_ffn_kernel, final_norm=final_g is not None),
        out_shape=jax.ShapeDtypeStruct((t, d), F32),
        grid=(t // tm, f // tf),
        in_specs=[pl.BlockSpec((tm, d), lambda i, j: (i, 0)),
                  pl.BlockSpec((1, 8, d), lambda i, j: (i // tiles_per_mod, 0, 0)),
                  pl.BlockSpec((1, d), lambda i, j: (0, 0)),
                  pl.BlockSpec((None, None, d, tf), lambda i, j: (layer, which, 0, j)),
                  pl.BlockSpec((None, None, d, tf), lambda i, j: (layer, which, 0, j)),
                  pl.BlockSpec((None, None, tf, d), lambda i, j: (layer, which, j, 0)),
                  pl.BlockSpec((1, d), lambda i, j: (0, 0))],
        out_specs=pl.BlockSpec((tm, d), lambda i, j: (i, 0)),
        scratch_shapes=[pltpu.VMEM((tm, d), BF16)],
        compiler_params=_cparams(("parallel", "arbitrary"), 56),
        name="ffn_half",
    )(x, modv, g.reshape(1, d), wg, wu, wd, fg)


def _mm_kernel(*refs, norm, resid):
    it = iter(refs)
    a_ref = next(it)
    b_ref = next(it)
    if norm:
        mod_ref, g_ref = next(it), next(it)
    elif resid:
        mod_ref = next(it)
    if resid:
        r_ref = next(it)
    o_ref = next(it)

    if norm:
        h_ref = next(it)

        @pl.when(pl.program_id(1) == 0)
        def _():
            _ada_norm_into(h_ref, a_ref, g_ref, mod_ref)
        a = h_ref[...]
    else:
        a = a_ref[...]
    acc = _dot(a, b_ref[...])
    if resid:
        acc = r_ref[...] + mod_ref[0, 2:3, :] * acc
    o_ref[...] = acc.astype(o_ref.dtype)


def _matmul(a, b, *, out_dtype, tm, tn, modv=None, norm_g=None, resid=None, vmem_mib=48,
            lhs_buffers=2, name="matmul"):
    m, k = a.shape
    n = b.shape[1]
    tm = min(tm, m if modv is None else m // modv.shape[0])
    norm = norm_g is not None
    has_res = resid is not None
    lhs_mode = {} if lhs_buffers == 2 else {"pipeline_mode": pl.Buffered(lhs_buffers)}
    in_specs = [pl.BlockSpec((tm, k), lambda i, j: (i, 0), **lhs_mode),
                pl.BlockSpec((k, tn), lambda i, j: (0, j))]
    args = [a, b]
    if norm or has_res:
        tiles_per_mod = m // tm // modv.shape[0]
        if norm:
            in_specs.append(pl.BlockSpec((1, 8, k), lambda i, j: (i // tiles_per_mod, 0, 0)))
            in_specs.append(pl.BlockSpec((1, k), lambda i, j: (0, 0)))
            args += [modv, norm_g.reshape(1, k)]
        else:
            in_specs.append(pl.BlockSpec((1, 8, tn), lambda i, j: (i // tiles_per_mod, 0, j)))
            args.append(modv)
    if has_res:
        in_specs.append(pl.BlockSpec((tm, tn), lambda i, j: (i, j)))
        args.append(resid)
    return pl.pallas_call(
        functools.partial(_mm_kernel, norm=norm, resid=has_res),
        out_shape=jax.ShapeDtypeStruct((m, n), out_dtype),
        grid=(m // tm, pl.cdiv(n, tn)),
        in_specs=in_specs,
        out_specs=pl.BlockSpec((tm, tn), lambda i, j: (i, j)),
        scratch_shapes=[pltpu.VMEM((tm, k), BF16)] if norm else [],
        compiler_params=_cparams(("parallel", "arbitrary"), vmem_mib),
        name=name,
    )(*args)


_CONV_ROWS = 256
_CONV_HALO = 8


def _conv_kernel(x_ref, w_ref, o_ref, xp_ref, *, seq, tc, l2norm, out_scale):
    zeros = jnp.zeros((_CONV_HALO, tc), F32)
    xp_ref[0:_CONV_HALO, :] = zeros
    xp_ref[_CONV_HALO + seq:, :] = zeros
    xp_ref[_CONV_HALO:_CONV_HALO + seq, :] = x_ref[0]
    w = w_ref[...]
    rows = min(_CONV_ROWS, seq)

    def step(t):
        base = t * rows
        acc = None
        for j in range(CONV_W):
            start = base + _CONV_HALO - CONV_W // 2 + j
            term = xp_ref[start:start + rows, :] * w[j:j + 1, :]
            acc = term if acc is None else acc + term
        y = _silu(acc)
        if l2norm:
            parts = []
            for hh in range(tc // LANES):
                yh = y[:, hh * LANES:(hh + 1) * LANES]
                ss = jnp.sum(yh * yh, axis=-1, keepdims=True)
                parts.append(yh * (lax.rsqrt(ss + EPS) * out_scale))
            y = parts[0] if len(parts) == 1 else jnp.concatenate(parts, axis=-1)
        o_ref[0, base:base + rows, :] = y.astype(o_ref.dtype)

    for t in range(seq // rows):
        step(t)


def _conv_act(p, conv_w, col0, ncols, *, l2norm, out_scale=1.0):
    b, s, _ = p.shape
    tc = 256
    off = col0 // tc
    return pl.pallas_call(
        functools.partial(_conv_kernel, seq=s, tc=tc, l2norm=l2norm, out_scale=out_scale),
        out_shape=jax.ShapeDtypeStruct((b, s, ncols), BF16),
        grid=(b, ncols // tc),
        in_specs=[pl.BlockSpec((1, s, tc), lambda i, j: (i, 0, j + off)),
                  pl.BlockSpec((CONV_W, tc), lambda i, j: (0, j + off))],
        out_specs=pl.BlockSpec((1, s, tc), lambda i, j: (i, 0, j)),
        scratch_shapes=[pltpu.VMEM((s + 2 * _CONV_HALO, tc), F32)],
        compiler_params=_cparams(("parallel", "parallel"), 40),
        name="dn_conv",
    )(p, conv_w)


def _split3(x):
    x1 = x.astype(BF16)
    r1 = x - x1.astype(F32)
    x2 = r1.astype(BF16)
    x3 = (r1 - x2.astype(F32)).astype(BF16)
    return x1, x2, x3


def _gates_kernel(ba_ref, a_ref, dt_ref, kind_ref, dir_ref, o_ref, *, chunk, tl):
    x = ba_ref[0]
    beta = jax.nn.sigmoid(x)
    y = x + dt_ref[...]
    softplus = jnp.maximum(y, 0.0) + jnp.log(1.0 + jnp.exp(-jnp.abs(y)))
    g = -jnp.exp(a_ref[...]) * softplus
    i = lax.broadcasted_iota(jnp.int32, (tl, tl), 0)
    j = lax.broadcasted_iota(jnp.int32, (tl, tl), 1)
    sh = int(math.log2(chunk))
    same = lax.shift_right_logical(i, sh) == lax.shift_right_logical(j, sh)
    pre = jnp.where(same & (i <= j), 1.0, 0.0).astype(BF16)
    suf = jnp.where(same & (i >= j), 1.0, 0.0).astype(BF16)
    g1, g2, g3 = _split3(g)
    cpre = _dot(g1, pre) + _dot(g2, pre) + _dot(g3, pre)
    csuf = _dot(g1, suf) + _dot(g2, suf) + _dot(g3, suf)
    cum = jnp.where(dir_ref[...] > 0.5, csuf, cpre)
    o_ref[0] = jnp.where(kind_ref[...] > 0.5, cum, beta)


def _gates(ba_t, a_col, dt_col, kind_col, dir_col, chunk):
    b, r, s = ba_t.shape
    tl = min(512, s)
    col = pl.BlockSpec((r, 1), lambda i, j: (0, 0))
    return pl.pallas_call(
        functools.partial(_gates_kernel, chunk=chunk, tl=tl),
        out_shape=jax.ShapeDtypeStruct((b, r, s), F32),
        grid=(b, s // tl),
        in_specs=[pl.BlockSpec((1, r, tl), lambda i, j: (i, 0, j)), col, col, col, col],
        out_specs=pl.BlockSpec((1, r, tl), lambda i, j: (i, 0, j)),
        compiler_params=_cparams(("parallel", "parallel"), 32),
        name="dn_gates",
    )(ba_t, a_col, dt_col, kind_col, dir_col)


def _col_form(gr_ref, dst_ref, n_rows, chunk):
    pad = jnp.zeros((LANES - 8, LANES), F32)
    for s in range(n_rows // LANES):
        if chunk <= LANES:
            per = LANES // chunk
            parts = [gr_ref[0, 0, s * per + t] for t in range(per)]
            slab = parts[0] if per == 1 else jnp.concatenate(parts, axis=-1)
        else:
            per = chunk // LANES
            slab = gr_ref[0, 0, s // per, :, (s % per) * LANES:(s % per + 1) * LANES]
        dst_ref[s * LANES:(s + 1) * LANES, :] = jnp.concatenate([slab, pad], axis=0).T


def _delta_kernel(qc_ref, kc_ref, vc_ref, grc_ref, ql_ref, kl_ref, vl_ref, grl_ref,
                  z_ref, ng_ref, o_ref, colc_ref, coll_ref, aq_ref, b_ref, gl_ref, s_ref,
                  oacc_ref, *, chunk, seq_c, seq_l):
    C = chunk
    n_c, n_l = seq_c // C, seq_l // C
    _col_form(grc_ref, colc_ref, seq_c, C)
    _col_form(grl_ref, coll_ref, seq_l, C)
    s_ref[...] = jnp.zeros_like(s_ref)
    oacc_ref[...] = jnp.zeros_like(oacc_ref)

    row = lax.broadcasted_iota(jnp.int32, (C, C), 0)
    col = lax.broadcasted_iota(jnp.int32, (C, C), 1)
    eye = jnp.where(row == col, 1.0, 0.0)

    def same_block(n):
        return lax.shift_right_logical(row, n) == lax.shift_right_logical(col, n)

    pair_mask = jnp.where(same_block(1), 1.0, 0.0)
    join_masks = [jnp.where(same_block(n + 1), 1.0, 0.0) - jnp.where(same_block(n), 1.0, 0.0)
                  for n in range(1, int(math.log2(C)))]

    chains = range(4)

    def take_rows(a, blk, d):
        off = blk if d == 0 else 0
        return jnp.concatenate([a[s + off:s + off + blk] for s in range(0, C, 2 * blk)], axis=0)

    def put_rows(a, new, blk, d):
        parts = []
        for i, s in enumerate(range(0, C, 2 * blk)):
            upd = new[i * blk:(i + 1) * blk]
            parts += [a[s:s + blk], upd] if d == 0 else [upd, a[s + blk:s + 2 * blk]]
        return jnp.concatenate(parts, axis=0)

    def phase_a(q_ref, k_ref, v_ref, gr_ref, col_ref, slot0, want_out, group):
        items = [(g, c) for g in range(group) for c in chains]

        def body(it, carry):
            cis = [it * group + g for g in range(group)]
            r0s = [pl.multiple_of(ci * C, C) for ci in cis]
            kb = [k_ref[0, pl.ds(r0, C), :] for r0 in r0s]
            kf = [t.astype(F32) for t in kb]
            gram = [_dot_nt(t, t) for t in kb]
            if want_out:
                qb = [q_ref[0, pl.ds(r0, C), :] for r0 in r0s]
                qk = [_dot_nt(qb[g], kb[g]) for g in range(group)]
            cols = [col_ref[pl.ds(r0, C), :] for r0 in r0s]
            beta, gcol, decay, lm, x = [], [], [], [], []
            for n, (g, c) in enumerate(items):
                d = c // 2
                beta.append(cols[g][:, c:c + 1])
                gcol.append(cols[g][:, 4 + c:5 + c])
                grow = gr_ref[0, 0, cis[g], 4 + c:5 + c, :]
                incl = (row >= col) if d == 0 else (row <= col)
                strict = (row > col) if d == 0 else (row < col)
                decay.append(jnp.exp(jnp.where(incl, gcol[n] - grow, -jnp.inf)))
                lm.append(jnp.where(strict, gram[g] * decay[n] * beta[n], 0.0))
                x.append(eye - lm[n] * pair_mask)
            idx = range(len(items))
            for lvl, jm in enumerate(join_masks, start=1):
                blk = 1 << lvl
                if blk < 8:
                    y = [_dot((lm[n] * jm).astype(BF16), x[n].astype(BF16)) for n in idx]
                    x = [x[n] - _dot(x[n].astype(BF16), y[n].astype(BF16)) for n in idx]
                    continue
                jmh = [take_rows(jm, blk, d) for d in range(2)]
                zeros = jnp.zeros((blk, C), F32)
                y = [_dot((take_rows(lm[n], blk, items[n][1] // 2)
                           * jmh[items[n][1] // 2]).astype(BF16), x[n].astype(BF16)) for n in idx]
                new = []
                for n in idx:
                    d = items[n][1] // 2
                    yfull = jnp.concatenate(
                        [t for s in range(C // (2 * blk))
                         for t in ((zeros, y[n][s * blk:(s + 1) * blk]) if d == 0
                                   else (y[n][s * blk:(s + 1) * blk], zeros))], axis=0)
                    xa = take_rows(x[n], blk, d)
                    new.append(xa - _dot(xa.astype(BF16), yfull.astype(BF16)))
                x = [put_rows(x[n], new[n], blk, items[n][1] // 2) for n in idx]
            eg = [jnp.exp(gcol[n]) for n in idx]
            sol = []
            for n, (g, c) in enumerate(items):
                e = c % 2
                vf = v_ref[0, pl.ds(r0s[g], C), e * DV:(e + 1) * DV].astype(F32)
                rhs = jnp.concatenate([vf * beta[n], kf[g] * (beta[n] * eg[n])], axis=-1)
                sol.append(_dot(x[n].astype(BF16), rhs.astype(BF16)).astype(BF16))
            kw = []
            for n, (g, c) in enumerate(items):
                glast = gcol[n][C - 1:C, :] if c // 2 == 0 else gcol[n][0:1, :]
                ktail = (kf[g] * jnp.exp(glast - gcol[n])).astype(BF16)
                kw.append(_dot_tn(ktail, sol[n]))
                gl_ref[c, slot0 + cis[g]] = jnp.broadcast_to(jnp.exp(glast), (8, LANES))
            if want_out:
                aw = [_dot((qk[g] * decay[n]).astype(BF16), sol[n])
                      for n, (g, c) in enumerate(items)]
            for n, (g, c) in enumerate(items):
                slot = slot0 + cis[g]
                b_ref[c, slot] = kw[n][:, :DV]
                aq_ref[c, slot, 0:DK, :] = (-kw[n][:, DV:]).astype(BF16)
                if want_out:
                    aq_ref[c, slot, DK:DK + C, :] = (qb[g].astype(F32) * eg[n]
                                                     - aw[n][:, DV:]).astype(BF16)
            if want_out:
                for g in range(group):
                    for e in range(2):
                        oacc_ref[pl.ds(r0s[g], C), e * DV:(e + 1) * DV] += (
                            aw[4 * g + e][:, :DV] + aw[4 * g + 2 + e][:, :DV])
            return carry
        return body

    def phase_b(n, slot0, want_out):
        def body(i, carry):
            states = [s_ref[c] for c in chains]
            cis = [i if c // 2 == 0 else n - 1 - i for c in chains]
            res = []
            for c in chains:
                slot = slot0 + cis[c]
                lhs = aq_ref[c, slot] if want_out else aq_ref[c, slot, 0:DK, :]
                res.append(_dot(lhs, states[c].astype(BF16)))
            for c in chains:
                slot = slot0 + cis[c]
                s_ref[c] = states[c] * gl_ref[c, slot, 0:1, :] + res[c][:DK] + b_ref[c, slot]
                if want_out:
                    r0 = pl.multiple_of(cis[c] * C, C)
                    e = c % 2
                    oacc_ref[pl.ds(r0, C), e * DV:(e + 1) * DV] += res[c][DK:]
            return carry
        return body

    grp_c = math.gcd(n_c, PHASE_A_GROUP)
    grp_l = math.gcd(n_l, PHASE_A_GROUP)
    lax.fori_loop(0, n_c // grp_c,
                  phase_a(qc_ref, kc_ref, vc_ref, grc_ref, colc_ref, 0, False, grp_c), 0)
    lax.fori_loop(0, n_l // grp_l,
                  phase_a(ql_ref, kl_ref, vl_ref, grl_ref, coll_ref, n_c, True, grp_l), 0)
    lax.fori_loop(0, n_c, phase_b(n_c, 0, False), 0)
    lax.fori_loop(0, n_l, phase_b(n_l, n_c, True), 0, unroll=4)

    rows = min(512, seq_l)

    def epilogue(t, carry):
        base = pl.multiple_of(t * rows, 8)
        for e in range(2):
            o = oacc_ref[pl.ds(base, rows), e * DV:(e + 1) * DV]
            ms = jnp.mean(o * o, axis=-1, keepdims=True)
            y = o * lax.rsqrt(ms + EPS) * ng_ref[...]
            zz = z_ref[0, pl.ds(base, rows), e * DV:(e + 1) * DV]
            o_ref[0, pl.ds(base, rows), e * DV:(e + 1) * DV] = (y * _silu(zz)).astype(o_ref.dtype)
        return carry

    lax.fori_loop(0, seq_l // rows, epilogue, 0)


def _delta(qc, kc, vc, grc, ql, kl, vl, grl, p_lat, z_col0, norm_g, chunk):
    b, seq_c, _ = qc.shape
    seq_l = ql.shape[1]
    n_c, n_l = seq_c // chunk, seq_l // chunk
    zoff = z_col0 // (2 * DV)

    def tok(seq, width):
        return pl.BlockSpec((1, seq, width), lambda i, h: (i, 0, h))

    def gate_spec(n):
        return pl.BlockSpec((1, 1, n, 8, chunk), lambda i, h: (i, h, 0, 0, 0))

    return pl.pallas_call(
        functools.partial(_delta_kernel, chunk=chunk, seq_c=seq_c, seq_l=seq_l),
        out_shape=jax.ShapeDtypeStruct((b, seq_l, HV * DV), BF16),
        grid=(b, HK),
        in_specs=[tok(seq_c, DK), tok(seq_c, DK), tok(seq_c, 2 * DV), gate_spec(n_c),
                  tok(seq_l, DK), tok(seq_l, DK), tok(seq_l, 2 * DV), gate_spec(n_l),
                  pl.BlockSpec((1, seq_l, 2 * DV), lambda i, h: (i, 0, h + zoff)),
                  pl.BlockSpec((1, DV), lambda i, h: (0, 0))],
        out_specs=tok(seq_l, 2 * DV),
        scratch_shapes=[pltpu.VMEM((seq_c, LANES), F32),
                        pltpu.VMEM((seq_l, LANES), F32),
                        pltpu.VMEM((4, n_c + n_l, DK + chunk, DV), BF16),
                        pltpu.VMEM((4, n_c + n_l, DK, DV), F32),
                        pltpu.VMEM((4, n_c + n_l, 8, LANES), F32),
                        pltpu.VMEM((4, DK, DV), F32),
                        pltpu.VMEM((seq_l, 2 * DV), F32)],
        compiler_params=_cparams(("parallel", "parallel"), 56),
        name="dn_delta",
    )(qc, kc, vc, grc, ql, kl, vl, grl, p_lat, norm_g.reshape(1, DV))


def _chan_dft_kernel(tab_ref, w_ref, o_ref):
    o_ref[...] = _dot(tab_ref[0], w_ref[...]).astype(o_ref.dtype)


def _chan_dft_weights(tabs, w):
    d = w.shape[0]
    g = tabs.shape[1]
    tn = 512
    nj = d // tn
    return pl.pallas_call(
        _chan_dft_kernel,
        out_shape=jax.ShapeDtypeStruct((d, 2 * d), BF16),
        grid=(d // g, 2, nj),
        in_specs=[pl.BlockSpec((1, g, g), lambda i, t, j: (t, 0, 0)),
                  pl.BlockSpec((g, tn), lambda i, t, j: (i, j))],
        out_specs=pl.BlockSpec((g, tn), lambda i, t, j: (i, t * nj + j)),
        compiler_params=_cparams(("parallel", "parallel", "parallel"), 32),
        name="fourier_chan",
    )(tabs, w)


def _dft_fold_kernel(lo_ref, hi_ref, ev_ref, od_ref, *, n, tr, d):
    lo = lo_ref[0].astype(F32)
    hi = hi_ref[0].astype(F32)
    ev_ref[0] = (lo + hi).astype(ev_ref.dtype)
    diff = lo - hi
    ad, bd = diff[:, :d], diff[:, d:]
    pos = pl.program_id(1) * tr + lax.broadcasted_iota(jnp.int32, (tr, 1), 0)
    theta = pos.astype(F32) * (2.0 * math.pi / n)
    c, s = jnp.cos(theta), jnp.sin(theta)
    od_ref[0, :, :d] = (ad * c - bd * s).astype(od_ref.dtype)
    od_ref[0, :, d:] = (bd * c + ad * s).astype(od_ref.dtype)


def _dft_fold(ab):
    b, n, d2 = ab.shape
    half = n // 2
    tr = min(256, half)
    nb = half // tr
    out = jax.ShapeDtypeStruct((b, half, d2), ab.dtype)
    spec = pl.BlockSpec((1, tr, d2), lambda i, t: (i, t, 0))
    return pl.pallas_call(
        functools.partial(_dft_fold_kernel, n=n, tr=tr, d=d2 // 2),
        out_shape=(out, out),
        grid=(b, nb),
        in_specs=[spec, pl.BlockSpec((1, tr, d2), lambda i, t: (i, t + nb, 0))],
        out_specs=(spec, spec),
        compiler_params=_cparams(("parallel", "parallel"), 40),
        name="fourier_fold",
    )(ab, ab)


def _seq_dft_kernel(cos_ref, sin_ref, ae_ref, be_ref, ao_ref, bo_ref, mod_ref, r_ref, o_ref,
                    mix_ref, *, scale, tm):
    cos_t, sin_t = cos_ref[...], sin_ref[...]
    gate = mod_ref[0, 2:3, :] * scale
    even = gate * (_dot(cos_t, ae_ref[0]) - _dot(sin_t, be_ref[0]))
    odd = gate * (_dot(cos_t, ao_ref[0]) - _dot(sin_t, bo_ref[0]))
    for q in range(mix_ref.shape[0]):
        lanes = slice(q * LANES, (q + 1) * LANES)
        mix_ref[q, pl.ds(0, tm, stride=2), :] = even[:, lanes]
        mix_ref[q, pl.ds(1, tm, stride=2), :] = odd[:, lanes]
        o_ref[0, :, lanes] = r_ref[0, :, lanes] + mix_ref[q]


def _seq_dft(cos_t, sin_t, ab_even, ab_odd, modv, resid, scale):
    b, half, d2 = ab_even.shape
    d = d2 // 2
    tm, tn = min(512, half), 512
    nj = d // tn
    tab = pl.BlockSpec((tm, half), lambda n, i, j: (i, 0))
    a_spec = pl.BlockSpec((1, half, tn), lambda n, i, j: (n, 0, j))
    b_spec = pl.BlockSpec((1, half, tn), lambda n, i, j: (n, 0, j + nj))
    io = pl.BlockSpec((1, 2 * tm, tn), lambda n, i, j: (n, i, j))
    return pl.pallas_call(
        functools.partial(_seq_dft_kernel, scale=scale, tm=tm),
        out_shape=jax.ShapeDtypeStruct((b, 2 * half, d), F32),
        grid=(b, half // tm, nj),
        in_specs=[tab, tab, a_spec, b_spec, a_spec, b_spec,
                  pl.BlockSpec((1, 8, tn), lambda n, i, j: (n, 0, j)), io],
        out_specs=io,
        scratch_shapes=[pltpu.VMEM((tn // LANES, 2 * tm, LANES), F32)],
        compiler_params=_cparams(("parallel", "parallel", "arbitrary"), 48),
        name="fourier_seq",
    )(cos_t, sin_t, ab_even, ab_even, ab_odd, ab_odd, modv, resid)


def _dft_table_kernel(cos_ref, sin_ref, tc_ref, ts_ref, *, n, tk):
    w = 2.0 * math.pi / n

    def unit(phase):
        ang = (phase & (n - 1)).astype(F32) * w
        return jnp.cos(ang), jnp.sin(ang)

    @pl.when(pl.program_id(0) == 0)
    def _():
        a = lax.broadcasted_iota(jnp.int32, (tk, n), 0)
        m = lax.broadcasted_iota(jnp.int32, (tk, n), 1)
        tc_ref[...], ts_ref[...] = unit(a * m)

    m1 = lax.broadcasted_iota(jnp.int32, (1, n), 1)
    rc, rs = unit((pl.program_id(0) * tk) * m1)
    tcv, tsv = tc_ref[...], ts_ref[...]
    cos_ref[...] = (rc * tcv - rs * tsv).astype(cos_ref.dtype)
    sin_ref[...] = (rs * tcv + rc * tsv).astype(sin_ref.dtype)


def _dft_tables(n):
    tk = min(256, n)
    out = jax.ShapeDtypeStruct((n, n), BF16)
    spec = pl.BlockSpec((tk, n), lambda i: (i, 0))
    return pl.pallas_call(
        functools.partial(_dft_table_kernel, n=n, tk=tk),
        out_shape=(out, out),
        grid=(n // tk,),
        in_specs=[],
        out_specs=(spec, spec),
        scratch_shapes=[pltpu.VMEM((tk, n), F32), pltpu.VMEM((tk, n), F32)],
        compiler_params=_cparams(("arbitrary",), 40),
        name="dft_tables",
    )()


def _gate_rows(p_ba, chunk):
    b, s, _ = p_ba.shape
    t = p_ba.reshape(b, s, 2, 2, HK, 2)
    return jnp.transpose(t, (0, 4, 2, 3, 5, 1)).reshape(b, 4 * HV, s)


def _gate_params(a_log, dt_bias):
    def rows(v, kind_fill):
        t = jnp.stack([jnp.full_like(v, kind_fill), v]).reshape(2, 2, HK, 2)
        return jnp.transpose(t, (2, 0, 1, 3)).reshape(4 * HV, 1)
    kind = rows(jnp.ones((2, HV), F32), 0.0)
    dirn = rows(jnp.stack([jnp.zeros((HV,), F32), jnp.ones((HV,), F32)]), 0.0)
    return rows(a_log.astype(F32), 0.0), rows(dt_bias.astype(F32), 0.0), kind, dirn


def _chunk_major(rows_t, chunk):
    b, _, s = rows_t.shape
    t = rows_t.reshape(b, HK, 8, s // chunk, chunk)
    return jnp.transpose(t, (0, 1, 3, 2, 4))


def kernel(x, c, ctx, c_ctx, norm_g, mod_w, mod_b, ffn_w_gate, ffn_w_up, ffn_w_down, dn_w_in,
           dn_conv_w, dn_a_log, dn_dt_bias, dn_norm_g, dn_w_out, fn_w_out, final_norm_g):
    bsz, seq, d = x.shape
    seq_c = ctx.shape[1]
    t_lat, t_ctx = bsz * seq, bsz * seq_c
    key_dim, val_dim = HK * DK, HV * DV
    qkv_dim = 2 * key_dim + val_dim
    chunk = DELTA_CHUNK

    cond = jnp.zeros((8, d), F32).at[:bsz].set(c).at[bsz].set(c_ctx)
    mod = _modulation(cond, mod_w, mod_b)

    def modv(layer, sub, ctx_rows=False):
        m3 = mod[layer, :, 3 * sub * d:3 * (sub + 1) * d].reshape(8, 3, d)
        m3 = m3[bsz:bsz + 1] if ctx_rows else m3[:bsz]
        return jnp.pad(m3, ((0, 0), (0, 5), (0, 0)))

    w_ffn = (ffn_w_gate, ffn_w_up, ffn_w_down)

    h = x.reshape(t_lat, d)
    hc = ctx.reshape(t_ctx, d)

    h = _ffn(h, modv(0, 0), norm_g[0, 0], *w_ffn, 0, 0)
    hc = _ffn(hc, modv(0, 0, True), norm_g[0, 0], *w_ffn, 0, 0)

    n_in = dn_w_in.shape[2]
    w_in = dn_w_in[0].astype(BF16)
    p = _matmul(h, w_in, out_dtype=F32, tm=2048, tn=512, modv=modv(0, 1), norm_g=norm_g[0, 1],
                lhs_buffers=1, name="dn_in_proj").reshape(bsz, seq, n_in)
    pc = _matmul(hc, w_in, out_dtype=F32, tm=512, tn=512, modv=modv(0, 1, True),
                 norm_g=norm_g[0, 1], name="dn_in_proj_ctx").reshape(bsz, seq_c, n_in)

    conv_w = dn_conv_w[0]
    a_col, dt_col, kind_col, dir_col = _gate_params(dn_a_log[0], dn_dt_bias[0])

    def dn_inputs(pp):
        q = _conv_act(pp, conv_w, 0, key_dim, l2norm=True, out_scale=DK ** -0.5)
        k = _conv_act(pp, conv_w, key_dim, key_dim, l2norm=True)
        v = _conv_act(pp, conv_w, 2 * key_dim, val_dim, l2norm=False)
        ba = pp[:, :, qkv_dim + val_dim:qkv_dim + val_dim + 4 * HV]
        gr = _gates(_gate_rows(ba, chunk), a_col, dt_col, kind_col, dir_col, chunk)
        return q, k, v, _chunk_major(gr, chunk)

    ql, kl, vl, grl = dn_inputs(p)
    qc, kc, vc, grc = dn_inputs(pc)
    y = _delta(qc, kc, vc, grc, ql, kl, vl, grl, p, qkv_dim, dn_norm_g[0], chunk)
    h = _matmul(y.reshape(t_lat, val_dim), dn_w_out[0].astype(BF16), out_dtype=F32, tm=1024,
                tn=512, modv=modv(0, 1), resid=h, name="dn_out_proj")
    h = _ffn(h, modv(0, 2), norm_g[0, 2], *w_ffn, 0, 1)

    h = _ffn(h, modv(1, 0), norm_g[1, 0], *w_ffn, 1, 0)
    gdim = d // N_FGROUPS
    cos_c, sin_c = _dft_tables(gdim)
    w_cs = _chan_dft_weights(jnp.stack([cos_c, sin_c]), fn_w_out[0].astype(BF16))
    ab = _matmul(h, w_cs, out_dtype=BF16, tm=2048, tn=1024, modv=modv(1, 1), norm_g=norm_g[1, 1],
                 lhs_buffers=1, vmem_mib=56, name="fourier_in").reshape(bsz, seq, 2 * d)
    cos_l, sin_l = _dft_tables(seq // 2)
    ab_even, ab_odd = _dft_fold(ab)
    h = _seq_dft(cos_l, sin_l, ab_even, ab_odd, modv(1, 1), h.reshape(bsz, seq, d),
                 1.0 / math.sqrt(seq * gdim)).reshape(t_lat, d)
    h = _ffn(h, modv(1, 2), norm_g[1, 2], *w_ffn, 1, 1, final_g=final_norm_g)
    return h.reshape(bsz, seq, d)
```

```python
import functools
import math

import jax
import jax.numpy as jnp
from jax import lax
from jax.experimental import pallas as pl
from jax.experimental.pallas import tpu as pltpu

F32 = jnp.float32
BF16 = jnp.bfloat16

EPS = 1e-6
FFN_RES = 0.5
N_SUB = 3
HK = 16
HV = 32
DK = 128
DV = 128
CONV_W = 5
N_FGROUPS = 4
DELTA_CHUNK = 128
PHASE_A_GROUP = 4

LANES = 128
MIB = 1024 * 1024


def _cparams(semantics, vmem_mib):
    return pltpu.CompilerParams(dimension_semantics=semantics,
                                vmem_limit_bytes=int(vmem_mib * MIB))


def _dot(a, b):
    return jnp.dot(a, b, preferred_element_type=F32)


def _dot_nt(a, b):
    return lax.dot_general(a, b, (((1,), (1,)), ((), ())), preferred_element_type=F32)


def _dot_tn(a, b):
    return lax.dot_general(a, b, (((0,), (0,)), ((), ())), preferred_element_type=F32)


def _silu(x):
    return x * jax.nn.sigmoid(x)


_NORM_ROWS = 32


def _ada_norm_into(h_ref, x_ref, g_ref, mod_ref):
    gain = g_ref[...] * (1.0 + mod_ref[0, 1:2, :])
    shift = mod_ref[0, 0:1, :]
    rows = min(_NORM_ROWS, x_ref.shape[0])

    def step(t, carry):
        r0 = pl.multiple_of(t * rows, rows)
        x = x_ref[pl.ds(r0, rows), :]
        ms = jnp.mean(x * x, axis=-1, keepdims=True)
        h_ref[pl.ds(r0, rows), :] = (x * lax.rsqrt(ms + EPS) * gain + shift).astype(BF16)
        return carry

    lax.fori_loop(0, x_ref.shape[0] // rows, step, 0, unroll=2)


def _mod_kernel(cond_ref, w_ref, b_ref, o_ref):
    s = _silu(cond_ref[...]).astype(BF16)
    o_ref[0] = _dot(s, w_ref[0].astype(BF16)) + b_ref[0]


def _modulation(cond, mod_w, mod_b):
    depth, d, n = mod_w.shape
    tn = 1024
    return pl.pallas_call(
        _mod_kernel,
        out_shape=jax.ShapeDtypeStruct((depth, 8, n), F32),
        grid=(depth, n // tn),
        in_specs=[pl.BlockSpec((8, d), lambda l, j: (0, 0)),
                  pl.BlockSpec((1, d, tn), lambda l, j: (l, 0, j)),
                  pl.BlockSpec((1, 1, tn), lambda l, j: (l, 0, j))],
        out_specs=pl.BlockSpec((1, 8, tn), lambda l, j: (l, 0, j)),
        compiler_params=_cparams(("parallel", "parallel"), 40),
        name="modulation",
    )(cond, mod_w, mod_b.reshape(depth, 1, n))


def _ffn_kernel(x_ref, mod_ref, g_ref, wg_ref, wu_ref, wd_ref, fg_ref, o_ref, h_ref, *,
                final_norm):
    j = pl.program_id(1)
    nj = pl.num_programs(1)

    @pl.when(j == 0)
    def _():
        _ada_norm_into(h_ref, x_ref, g_ref, mod_ref)
        o_ref[...] = jnp.zeros_like(o_ref)

    h = h_ref[...]
    a = (_silu(_dot(h, wg_ref[...].astype(BF16)))
         * _dot(h, wu_ref[...].astype(BF16))).astype(BF16)
    o_ref[...] += _dot(a, wd_ref[...].astype(BF16))

    @pl.when(j == nj - 1)
    def _():
        gate = FFN_RES * mod_ref[0, 2:3, :]
        rows = min(_NORM_ROWS, x_ref.shape[0])

        def step(t, carry):
            r0 = pl.multiple_of(t * rows, rows)
            r = x_ref[pl.ds(r0, rows), :] + gate * o_ref[pl.ds(r0, rows), :]
            if final_norm:
                ms = jnp.mean(r * r, axis=-1, keepdims=True)
                r = r * lax.rsqrt(ms + EPS) * fg_ref[...]
            o_ref[pl.ds(r0, rows), :] = r
            return carry

        lax.fori_loop(0, x_ref.shape[0] // rows, step, 0, unroll=2)


def _ffn(x, modv, g, wg, wu, wd, layer, which, final_g=None):
    t, d = x.shape
    f = wg.shape[3]
    tm = min(1024, t // modv.shape[0])
    tf = 256
    tiles_per_mod = t // tm // modv.shape[0]
    fg = (final_g if final_g is not None else g).reshape(1, d)
    return pl.pallas_call(
        functools.partial(_ffn_kernel, final_norm=final_g is not None),
        out_shape=jax.ShapeDtypeStruct((t, d), F32),
        grid=(t // tm, f // tf),
        in_specs=[pl.BlockSpec((tm, d), lambda i, j: (i, 0)),
                  pl.BlockSpec((1, 8, d), lambda i, j: (i // tiles_per_mod, 0, 0)),
                  pl.BlockSpec((1, d), lambda i, j: (0, 0)),
                  pl.BlockSpec((None, None, d, tf), lambda i, j: (layer, which, 0, j)),
                  pl.BlockSpec((None, None, d, tf), lambda i, j: (layer, which, 0, j)),
                  pl.BlockSpec((None, None, tf, d), lambda i, j: (layer, which, j, 0)),
                  pl.BlockSpec((1, d), lambda i, j: (0, 0))],
        out_specs=pl.BlockSpec((tm, d), lambda i, j: (i, 0)),
        scratch_shapes=[pltpu.VMEM((tm, d), BF16)],
        compiler_params=_cparams(("parallel", "arbitrary"), 56),
        name="ffn_half",
    )(x, modv, g.reshape(1, d), wg, wu, wd, fg)


def _mm_kernel(*refs, norm, resid):
    it = iter(refs)
    a_ref = next(it)
    b_ref = next(it)
    if norm:
        mod_ref, g_ref = next(it), next(it)
    elif resid:
        mod_ref = next(it)
    if resid:
        r_ref = next(it)
    o_ref = next(it)

    if norm:
        h_ref = next(it)

        @pl.when(pl.program_id(1) == 0)
        def _():
            _ada_norm_into(h_ref, a_ref, g_ref, mod_ref)
        a = h_ref[...]
    else:
        a = a_ref[...]
    acc = _dot(a, b_ref[...])
    if resid:
        acc = r_ref[...] + mod_ref[0, 2:3, :] * acc
    o_ref[...] = acc.astype(o_ref.dtype)


def _matmul(a, b, *, out_dtype, tm, tn, modv=None, norm_g=None, resid=None, vmem_mib=48,
            lhs_buffers=2, name="matmul"):
    m, k = a.shape
    n = b.shape[1]
    tm = min(tm, m if modv is None else m // modv.shape[0])
    norm = norm_g is not None
    has_res = resid is not None
    lhs_mode = {} if lhs_buffers == 2 else {"pipeline_mode": pl.Buffered(lhs_buffers)}
    in_specs = [pl.BlockSpec((tm, k), lambda i, j: (i, 0), **lhs_mode),
                pl.BlockSpec((k, tn), lambda i, j: (0, j))]
    args = [a, b]
    if norm or has_res:
        tiles_per_mod = m // tm // modv.shape[0]
        if norm:
            in_specs.append(pl.BlockSpec((1, 8, k), lambda i, j: (i // tiles_per_mod, 0, 0)))
            in_specs.append(pl.BlockSpec((1, k), lambda i, j: (0, 0)))
            args += [modv, norm_g.reshape(1, k)]
        else:
            in_specs.append(pl.BlockSpec((1, 8, tn), lambda i, j: (i // tiles_per_mod, 0, j)))
            args.append(modv)
    if has_res:
        in_specs.append(pl.BlockSpec((tm, tn), lambda i, j: (i, j)))
        args.append(resid)
    return pl.pallas_call(
        functools.partial(_mm_kernel, norm=norm, resid=has_res),
        out_shape=jax.ShapeDtypeStruct((m, n), out_dtype),
        grid=(m // tm, pl.cdiv(n, tn)),
        in_specs=in_specs,
        out_specs=pl.BlockSpec((tm, tn), lambda i, j: (i, j)),
        scratch_shapes=[pltpu.VMEM((tm, k), BF16)] if norm else [],
        compiler_params=_cparams(("parallel", "arbitrary"), vmem_mib),
        name=name,
    )(*args)


_CONV_ROWS = 256
_CONV_HALO = 8


def _conv_kernel(x_ref, w_ref, o_ref, xp_ref, *, seq, tc, l2norm, out_scale):
    zeros = jnp.zeros((_CONV_HALO, tc), F32)
    xp_ref[0:_CONV_HALO, :] = zeros
    xp_ref[_CONV_HALO + seq:, :] = zeros
    xp_ref[_CONV_HALO:_CONV_HALO + seq, :] = x_ref[0].astype(F32)
    w = w_ref[...]
    rows = min(_CONV_ROWS, seq)

    def step(t):
        base = t * rows
        acc = None
        for j in range(CONV_W):
            start = base + _CONV_HALO - CONV_W // 2 + j
            term = xp_ref[start:start + rows, :] * w[j:j + 1, :]
            acc = term if acc is None else acc + term
        y = _silu(acc)
        if l2norm:
            parts = []
            for hh in range(tc // LANES):
                yh = y[:, hh * LANES:(hh + 1) * LANES]
                ss = jnp.sum(yh * yh, axis=-1, keepdims=True)
                parts.append(yh * (lax.rsqrt(ss + EPS) * out_scale))
            y = parts[0] if len(parts) == 1 else jnp.concatenate(parts, axis=-1)
        o_ref[0, base:base + rows, :] = y.astype(o_ref.dtype)

    for t in range(seq // rows):
        step(t)


def _conv_act(p, conv_w, col0, ncols, *, l2norm, out_scale=1.0):
    b, s, _ = p.shape
    tc = 256
    off = col0 // tc
    return pl.pallas_call(
        functools.partial(_conv_kernel, seq=s, tc=tc, l2norm=l2norm, out_scale=out_scale),
        out_shape=jax.ShapeDtypeStruct((b, s, ncols), BF16),
        grid=(b, ncols // tc),
        in_specs=[pl.BlockSpec((1, s, tc), lambda i, j: (i, 0, j + off)),
                  pl.BlockSpec((CONV_W, tc), lambda i, j: (0, j + off))],
        out_specs=pl.BlockSpec((1, s, tc), lambda i, j: (i, 0, j)),
        scratch_shapes=[pltpu.VMEM((s + 2 * _CONV_HALO, tc), F32)],
        compiler_params=_cparams(("parallel", "parallel"), 40),
        name="dn_conv",
    )(p, conv_w)


def _split3(x):
    x1 = x.astype(BF16)
    r1 = x - x1.astype(F32)
    x2 = r1.astype(BF16)
    x3 = (r1 - x2.astype(F32)).astype(BF16)
    return x1, x2, x3


def _gates_kernel(ba_ref, a_ref, dt_ref, kind_ref, dir_ref, o_ref, *, chunk, tl):
    x = ba_ref[0]
    beta = jax.nn.sigmoid(x)
    y = x + dt_ref[...]
    softplus = jnp.maximum(y, 0.0) + jnp.log(1.0 + jnp.exp(-jnp.abs(y)))
    g = -jnp.exp(a_ref[...]) * softplus
    i = lax.broadcasted_iota(jnp.int32, (tl, tl), 0)
    j = lax.broadcasted_iota(jnp.int32, (tl, tl), 1)
    sh = int(math.log2(chunk))
    same = lax.shift_right_logical(i, sh) == lax.shift_right_logical(j, sh)
    pre = jnp.where(same & (i <= j), 1.0, 0.0).astype(BF16)
    suf = jnp.where(same & (i >= j), 1.0, 0.0).astype(BF16)
    g1, g2, g3 = _split3(g)
    cpre = _dot(g1, pre) + _dot(g2, pre) + _dot(g3, pre)
    csuf = _dot(g1, suf) + _dot(g2, suf) + _dot(g3, suf)
    cum = jnp.where(dir_ref[...] > 0.5, csuf, cpre)
    o_ref[0] = jnp.where(kind_ref[...] > 0.5, cum, beta)


def _gates(ba_t, a_col, dt_col, kind_col, dir_col, chunk):
    b, r, s = ba_t.shape
    tl = min(512, s)
    col = pl.BlockSpec((r, 1), lambda i, j: (0, 0))
    return pl.pallas_call(
        functools.partial(_gates_kernel, chunk=chunk, tl=tl),
        out_shape=jax.ShapeDtypeStruct((b, r, s), F32),
        grid=(b, s // tl),
        in_specs=[pl.BlockSpec((1, r, tl), lambda i, j: (i, 0, j)), col, col, col, col],
        out_specs=pl.BlockSpec((1, r, tl), lambda i, j: (i, 0, j)),
        compiler_params=_cparams(("parallel", "parallel"), 32),
        name="dn_gates",
    )(ba_t, a_col, dt_col, kind_col, dir_col)


def _col_form(gr_ref, dst_ref, n_rows, chunk):
    pad = jnp.zeros((LANES - 8, LANES), F32)
    for s in range(n_rows // LANES):
        if chunk <= LANES:
            per = LANES // chunk
            parts = [gr_ref[0, 0, s * per + t] for t in range(per)]
            slab = parts[0] if per == 1 else jnp.concatenate(parts, axis=-1)
        else:
            per = chunk // LANES
            slab = gr_ref[0, 0, s // per, :, (s % per) * LANES:(s % per + 1) * LANES]
        dst_ref[s * LANES:(s + 1) * LANES, :] = jnp.concatenate([slab, pad], axis=0).T


def _delta_kernel(qc_ref, kc_ref, vc_ref, grc_ref, ql_ref, kl_ref, vl_ref, grl_ref,
                  z_ref, ng_ref, o_ref, colc_ref, coll_ref, aq_ref, b_ref, gl_ref, s_ref,
                  oacc_ref, *, chunk, seq_c, seq_l):
    C = chunk
    n_c, n_l = seq_c // C, seq_l // C
    _col_form(grc_ref, colc_ref, seq_c, C)
    _col_form(grl_ref, coll_ref, seq_l, C)
    s_ref[...] = jnp.zeros_like(s_ref)
    oacc_ref[...] = jnp.zeros_like(oacc_ref)

    row = lax.broadcasted_iota(jnp.int32, (C, C), 0)
    col = lax.broadcasted_iota(jnp.int32, (C, C), 1)
    eye = jnp.where(row == col, 1.0, 0.0)

    def same_block(n):
        return lax.shift_right_logical(row, n) == lax.shift_right_logical(col, n)

    pair_mask = jnp.where(same_block(1), 1.0, 0.0)
    join_masks = [jnp.where(same_block(n + 1), 1.0, 0.0) - jnp.where(same_block(n), 1.0, 0.0)
                  for n in range(1, int(math.log2(C)))]

    chains = range(4)

    def take_rows(a, blk, d):
        off = blk if d == 0 else 0
        return jnp.concatenate([a[s + off:s + off + blk] for s in range(0, C, 2 * blk)], axis=0)

    def put_rows(a, new, blk, d):
        parts = []
        for i, s in enumerate(range(0, C, 2 * blk)):
            upd = new[i * blk:(i + 1) * blk]
            parts += [a[s:s + blk], upd] if d == 0 else [upd, a[s + blk:s + 2 * blk]]
        return jnp.concatenate(parts, axis=0)

    def phase_a(q_ref, k_ref, v_ref, gr_ref, col_ref, slot0, want_out, group):
        items = [(g, c) for g in range(group) for c in chains]

        def body(it, carry):
            cis = [it * group + g for g in range(group)]
            r0s = [pl.multiple_of(ci * C, C) for ci in cis]
            kb = [k_ref[0, pl.ds(r0, C), :] for r0 in r0s]
            kf = [t.astype(F32) for t in kb]
            gram = [_dot_nt(t, t) for t in kb]
            if want_out:
                qb = [q_ref[0, pl.ds(r0, C), :] for r0 in r0s]
                qk = [_dot_nt(qb[g], kb[g]) for g in range(group)]
            cols = [col_ref[pl.ds(r0, C), :] for r0 in r0s]
            beta, gcol, decay, lm, x = [], [], [], [], []
            for n, (g, c) in enumerate(items):
                d = c // 2
                beta.append(cols[g][:, c:c + 1])
                gcol.append(cols[g][:, 4 + c:5 + c])
                grow = gr_ref[0, 0, cis[g], 4 + c:5 + c, :]
                incl = (row >= col) if d == 0 else (row <= col)
                strict = (row > col) if d == 0 else (row < col)
                decay.append(jnp.exp(jnp.where(incl, gcol[n] - grow, -jnp.inf)))
                lm.append(jnp.where(strict, gram[g] * decay[n] * beta[n], 0.0))
                x.append(eye - lm[n] * pair_mask)
            idx = range(len(items))
            for lvl, jm in enumerate(join_masks, start=1):
                blk = 1 << lvl
                if blk < 8:
                    y = [_dot((lm[n] * jm).astype(BF16), x[n].astype(BF16)) for n in idx]
                    x = [x[n] - _dot(x[n].astype(BF16), y[n].astype(BF16)) for n in idx]
                    continue
                jmh = [take_rows(jm, blk, d) for d in range(2)]
                zeros = jnp.zeros((blk, C), F32)
                y = [_dot((take_rows(lm[n], blk, items[n][1] // 2)
                           * jmh[items[n][1] // 2]).astype(BF16), x[n].astype(BF16)) for n in idx]
                new = []
                for n in idx:
                    d = items[n][1] // 2
                    yfull = jnp.concatenate(
                        [t for s in range(C // (2 * blk))
                         for t in ((zeros, y[n][s * blk:(s + 1) * blk]) if d == 0
                                   else (y[n][s * blk:(s + 1) * blk], zeros))], axis=0)
                    xa = take_rows(x[n], blk, d)
                    new.append(xa - _dot(xa.astype(BF16), yfull.astype(BF16)))
                x = [put_rows(x[n], new[n], blk, items[n][1] // 2) for n in idx]
            eg = [jnp.exp(gcol[n]) for n in idx]
            sol = []
            for n, (g, c) in enumerate(items):
                e = c % 2
                vf = v_ref[0, pl.ds(r0s[g], C), e * DV:(e + 1) * DV].astype(F32)
                rhs = jnp.concatenate([vf * beta[n], kf[g] * (beta[n] * eg[n])], axis=-1)
                sol.append(_dot(x[n].astype(BF16), rhs.astype(BF16)).astype(BF16))
            kw = []
            for n, (g, c) in enumerate(items):
                glast = gcol[n][C - 1:C, :] if c // 2 == 0 else gcol[n][0:1, :]
                ktail = (kf[g] * jnp.exp(glast - gcol[n])).astype(BF16)
                kw.append(_dot_tn(ktail, sol[n]))
                gl_ref[c, slot0 + cis[g]] = jnp.broadcast_to(jnp.exp(glast), (8, LANES))
            if want_out:
                aw = [_dot((qk[g] * decay[n]).astype(BF16), sol[n])
                      for n, (g, c) in enumerate(items)]
            for n, (g, c) in enumerate(items):
                slot = slot0 + cis[g]
                b_ref[c, slot] = kw[n][:, :DV]
                aq_ref[c, slot, 0:DK, :] = (-kw[n][:, DV:]).astype(BF16)
                if want_out:
                    aq_ref[c, slot, DK:DK + C, :] = (qb[g].astype(F32) * eg[n]
                                                     - aw[n][:, DV:]).astype(BF16)
            if want_out:
                for g in range(group):
                    for e in range(2):
                        oacc_ref[pl.ds(r0s[g], C), e * DV:(e + 1) * DV] += (
                            aw[4 * g + e][:, :DV] + aw[4 * g + 2 + e][:, :DV])
            return carry
        return body

    def phase_b(n, slot0, want_out):
        def body(i, carry):
            states = [s_ref[c] for c in chains]
            cis = [i if c // 2 == 0 else n - 1 - i for c in chains]
            res = []
            for c in chains:
                slot = slot0 + cis[c]
                lhs = aq_ref[c, slot] if want_out else aq_ref[c, slot, 0:DK, :]
                res.append(_dot(lhs, states[c].astype(BF16)))
            for c in chains:
                slot = slot0 + cis[c]
                s_ref[c] = states[c] * gl_ref[c, slot, 0:1, :] + res[c][:DK] + b_ref[c, slot]
                if want_out:
                    r0 = pl.multiple_of(cis[c] * C, C)
                    e = c % 2
                    oacc_ref[pl.ds(r0, C), e * DV:(e + 1) * DV] += res[c][DK:]
            return carry
        return body

    grp_c = math.gcd(n_c, PHASE_A_GROUP)
    grp_l = math.gcd(n_l, PHASE_A_GROUP)
    lax.fori_loop(0, n_c // grp_c,
                  phase_a(qc_ref, kc_ref, vc_ref, grc_ref, colc_ref, 0, False, grp_c), 0)
    lax.fori_loop(0, n_l // grp_l,
                  phase_a(ql_ref, kl_ref, vl_ref, grl_ref, coll_ref, n_c, True, grp_l), 0)
    lax.fori_loop(0, n_c, phase_b(n_c, 0, False), 0)
    lax.fori_loop(0, n_l, phase_b(n_l, n_c, True), 0, unroll=4)

    rows = min(512, seq_l)

    def epilogue(t, carry):
        base = pl.multiple_of(t * rows, 8)
        for e in range(2):
            o = oacc_ref[pl.ds(base, rows), e * DV:(e + 1) * DV]
            ms = jnp.mean(o * o, axis=-1, keepdims=True)
            y = o * lax.rsqrt(ms + EPS) * ng_ref[...]
            zz = z_ref[0, pl.ds(base, rows), e * DV:(e + 1) * DV].astype(F32)
            o_ref[0, pl.ds(base, rows), e * DV:(e + 1) * DV] = (y * _silu(zz)).astype(o_ref.dtype)
        return carry

    lax.fori_loop(0, seq_l // rows, epilogue, 0)


def _delta(qc, kc, vc, grc, ql, kl, vl, grl, p_lat, z_col0, norm_g, chunk):
    b, seq_c, _ = qc.shape
    seq_l = ql.shape[1]
    n_c, n_l = seq_c // chunk, seq_l // chunk
    zoff = z_col0 // (2 * DV)

    def tok(seq, width):
        return pl.BlockSpec((1, seq, width), lambda i, h: (i, 0, h))

    def gate_spec(n):
        return pl.BlockSpec((1, 1, n, 8, chunk), lambda i, h: (i, h, 0, 0, 0))

    return pl.pallas_call(
        functools.partial(_delta_kernel, chunk=chunk, seq_c=seq_c, seq_l=seq_l),
        out_shape=jax.ShapeDtypeStruct((b, seq_l, HV * DV), BF16),
        grid=(b, HK),
        in_specs=[tok(seq_c, DK), tok(seq_c, DK), tok(seq_c, 2 * DV), gate_spec(n_c),
                  tok(seq_l, DK), tok(seq_l, DK), tok(seq_l, 2 * DV), gate_spec(n_l),
                  pl.BlockSpec((1, seq_l, 2 * DV), lambda i, h: (i, 0, h + zoff)),
                  pl.BlockSpec((1, DV), lambda i, h: (0, 0))],
        out_specs=tok(seq_l, 2 * DV),
        scratch_shapes=[pltpu.VMEM((seq_c, LANES), F32),
                        pltpu.VMEM((seq_l, LANES), F32),
                        pltpu.VMEM((4, n_c + n_l, DK + chunk, DV), BF16),
                        pltpu.VMEM((4, n_c + n_l, DK, DV), F32),
                        pltpu.VMEM((4, n_c + n_l, 8, LANES), F32),
                        pltpu.VMEM((4, DK, DV), F32),
                        pltpu.VMEM((seq_l, 2 * DV), F32)],
        compiler_params=_cparams(("parallel", "parallel"), 56),
        name="dn_delta",
    )(qc, kc, vc, grc, ql, kl, vl, grl, p_lat, norm_g.reshape(1, DV))


def _chan_dft_kernel(tab_ref, w_ref, o_ref):
    o_ref[...] = _dot(tab_ref[0], w_ref[...]).astype(o_ref.dtype)


def _chan_dft_weights(tabs, w):
    d = w.shape[0]
    g = tabs.shape[1]
    tn = 512
    nj = d // tn
    return pl.pallas_call(
        _chan_dft_kernel,
        out_shape=jax.ShapeDtypeStruct((d, 2 * d), BF16),
        grid=(d // g, 2, nj),
        in_specs=[pl.BlockSpec((1, g, g), lambda i, t, j: (t, 0, 0)),
                  pl.BlockSpec((g, tn), lambda i, t, j: (i, j))],
        out_specs=pl.BlockSpec((g, tn), lambda i, t, j: (i, t * nj + j)),
        compiler_params=_cparams(("parallel", "parallel", "parallel"), 32),
        name="fourier_chan",
    )(tabs, w)


def _dft_fold_kernel(lo_ref, hi_ref, ev_ref, od_ref, *, n, tr, d):
    lo = lo_ref[0].astype(F32)
    hi = hi_ref[0].astype(F32)
    ev_ref[0] = (lo + hi).astype(ev_ref.dtype)
    diff = lo - hi
    ad, bd = diff[:, :d], diff[:, d:]
    pos = pl.program_id(1) * tr + lax.broadcasted_iota(jnp.int32, (tr, 1), 0)
    theta = pos.astype(F32) * (2.0 * math.pi / n)
    c, s = jnp.cos(theta), jnp.sin(theta)
    od_ref[0, :, :d] = (ad * c - bd * s).astype(od_ref.dtype)
    od_ref[0, :, d:] = (bd * c + ad * s).astype(od_ref.dtype)


def _dft_fold(ab):
    b, n, d2 = ab.shape
    half = n // 2
    tr = min(256, half)
    nb = half // tr
    out = jax.ShapeDtypeStruct((b, half, d2), ab.dtype)
    spec = pl.BlockSpec((1, tr, d2), lambda i, t: (i, t, 0))
    return pl.pallas_call(
        functools.partial(_dft_fold_kernel, n=n, tr=tr, d=d2 // 2),
        out_shape=(out, out),
        grid=(b, nb),
        in_specs=[spec, pl.BlockSpec((1, tr, d2), lambda i, t: (i, t + nb, 0))],
        out_specs=(spec, spec),
        compiler_params=_cparams(("parallel", "parallel"), 40),
        name="fourier_fold",
    )(ab, ab)


def _seq_dft_kernel(cos_ref, sin_ref, ae_ref, be_ref, ao_ref, bo_ref, mod_ref, r_ref, o_ref,
                    mix_ref, *, scale, tm):
    cos_t, sin_t = cos_ref[...], sin_ref[...]
    gate = mod_ref[0, 2:3, :] * scale
    even = gate * (_dot(cos_t, ae_ref[0]) - _dot(sin_t, be_ref[0]))
    odd = gate * (_dot(cos_t, ao_ref[0]) - _dot(sin_t, bo_ref[0]))
    for q in range(mix_ref.shape[0]):
        lanes = slice(q * LANES, (q + 1) * LANES)
        mix_ref[q, pl.ds(0, tm, stride=2), :] = even[:, lanes]
        mix_ref[q, pl.ds(1, tm, stride=2), :] = odd[:, lanes]
        o_ref[0, :, lanes] = r_ref[0, :, lanes] + mix_ref[q]


def _seq_dft(cos_t, sin_t, ab_even, ab_odd, modv, resid, scale):
    b, half, d2 = ab_even.shape
    d = d2 // 2
    tm, tn = min(512, half), 512
    nj = d // tn
    tab = pl.BlockSpec((tm, half), lambda n, i, j: (i, 0))
    a_spec = pl.BlockSpec((1, half, tn), lambda n, i, j: (n, 0, j))
    b_spec = pl.BlockSpec((1, half, tn), lambda n, i, j: (n, 0, j + nj))
    io = pl.BlockSpec((1, 2 * tm, tn), lambda n, i, j: (n, i, j))
    return pl.pallas_call(
        functools.partial(_seq_dft_kernel, scale=scale, tm=tm),
        out_shape=jax.ShapeDtypeStruct((b, 2 * half, d), F32),
        grid=(b, half // tm, nj),
        in_specs=[tab, tab, a_spec, b_spec, a_spec, b_spec,
                  pl.BlockSpec((1, 8, tn), lambda n, i, j: (n, 0, j)), io],
        out_specs=io,
        scratch_shapes=[pltpu.VMEM((tn // LANES, 2 * tm, LANES), F32)],
        compiler_params=_cparams(("parallel", "parallel", "arbitrary"), 48),
        name="fourier_seq",
    )(cos_t, sin_t, ab_even, ab_even, ab_odd, ab_odd, modv, resid)


def _dft_table_kernel(cos_ref, sin_ref, tc_ref, ts_ref, *, n, tk):
    w = 2.0 * math.pi / n

    def unit(phase):
        ang = (phase & (n - 1)).astype(F32) * w
        return jnp.cos(ang), jnp.sin(ang)

    @pl.when(pl.program_id(0) == 0)
    def _():
        a = lax.broadcasted_iota(jnp.int32, (tk, n), 0)
        m = lax.broadcasted_iota(jnp.int32, (tk, n), 1)
        tc_ref[...], ts_ref[...] = unit(a * m)

    m1 = lax.broadcasted_iota(jnp.int32, (1, n), 1)
    rc, rs = unit((pl.program_id(0) * tk) * m1)
    tcv, tsv = tc_ref[...], ts_ref[...]
    cos_ref[...] = (rc * tcv - rs * tsv).astype(cos_ref.dtype)
    sin_ref[...] = (rs * tcv + rc * tsv).astype(sin_ref.dtype)


def _dft_tables(n):
    tk = min(256, n)
    out = jax.ShapeDtypeStruct((n, n), BF16)
    spec = pl.BlockSpec((tk, n), lambda i: (i, 0))
    return pl.pallas_call(
        functools.partial(_dft_table_kernel, n=n, tk=tk),
        out_shape=(out, out),
        grid=(n // tk,),
        in_specs=[],
        out_specs=(spec, spec),
        scratch_shapes=[pltpu.VMEM((tk, n), F32), pltpu.VMEM((tk, n), F32)],
        compiler_params=_cparams(("arbitrary",), 40),
        name="dft_tables",
    )()


def _gate_rows(p_ba, chunk):
    b, s, _ = p_ba.shape
    t = p_ba.astype(F32).reshape(b, s, 2, 2, HK, 2)
    return jnp.transpose(t, (0, 4, 2, 3, 5, 1)).reshape(b, 4 * HV, s)


def _gate_params(a_log, dt_bias):
    def rows(v, kind_fill):
        t = jnp.stack([jnp.full_like(v, kind_fill), v]).reshape(2, 2, HK, 2)
        return jnp.transpose(t, (2, 0, 1, 3)).reshape(4 * HV, 1)
    kind = rows(jnp.ones((2, HV), F32), 0.0)
    dirn = rows(jnp.stack([jnp.zeros((HV,), F32), jnp.ones((HV,), F32)]), 0.0)
    return rows(a_log.astype(F32), 0.0), rows(dt_bias.astype(F32), 0.0), kind, dirn


def _chunk_major(rows_t, chunk):
    b, _, s = rows_t.shape
    t = rows_t.reshape(b, HK, 8, s // chunk, chunk)
    return jnp.transpose(t, (0, 1, 3, 2, 4))


def kernel(x, c, ctx, c_ctx, norm_g, mod_w, mod_b, ffn_w_gate, ffn_w_up, ffn_w_down, dn_w_in,
           dn_conv_w, dn_a_log, dn_dt_bias, dn_norm_g, dn_w_out, fn_w_out, final_norm_g):
    bsz, seq, d = x.shape
    seq_c = ctx.shape[1]
    t_lat, t_ctx = bsz * seq, bsz * seq_c
    key_dim, val_dim = HK * DK, HV * DV
    qkv_dim = 2 * key_dim + val_dim
    chunk = DELTA_CHUNK

    cond = jnp.zeros((8, d), F32).at[:bsz].set(c).at[bsz].set(c_ctx)
    mod = _modulation(cond, mod_w, mod_b)

    def modv(layer, sub, ctx_rows=False):
        m3 = mod[layer, :, 3 * sub * d:3 * (sub + 1) * d].reshape(8, 3, d)
        m3 = m3[bsz:bsz + 1] if ctx_rows else m3[:bsz]
        return jnp.pad(m3, ((0, 0), (0, 5), (0, 0)))

    w_ffn = (ffn_w_gate, ffn_w_up, ffn_w_down)

    h = x.reshape(t_lat, d)
    hc = ctx.reshape(t_ctx, d)

    h = _ffn(h, modv(0, 0), norm_g[0, 0], *w_ffn, 0, 0)
    hc = _ffn(hc, modv(0, 0, True), norm_g[0, 0], *w_ffn, 0, 0)

    n_in = dn_w_in.shape[2]
    w_in = dn_w_in[0].astype(BF16)
    p = _matmul(h, w_in, out_dtype=BF16, tm=2048, tn=512, modv=modv(0, 1), norm_g=norm_g[0, 1],
                lhs_buffers=1, name="dn_in_proj").reshape(bsz, seq, n_in)
    pc = _matmul(hc, w_in, out_dtype=BF16, tm=512, tn=512, modv=modv(0, 1, True),
                 norm_g=norm_g[0, 1], name="dn_in_proj_ctx").reshape(bsz, seq_c, n_in)

    conv_w = dn_conv_w[0]
    a_col, dt_col, kind_col, dir_col = _gate_params(dn_a_log[0], dn_dt_bias[0])

    def dn_inputs(pp):
        q = _conv_act(pp, conv_w, 0, key_dim, l2norm=True, out_scale=DK ** -0.5)
        k = _conv_act(pp, conv_w, key_dim, key_dim, l2norm=True)
        v = _conv_act(pp, conv_w, 2 * key_dim, val_dim, l2norm=False)
        ba = pp[:, :, qkv_dim + val_dim:qkv_dim + val_dim + 4 * HV]
        gr = _gates(_gate_rows(ba, chunk), a_col, dt_col, kind_col, dir_col, chunk)
        return q, k, v, _chunk_major(gr, chunk)

    ql, kl, vl, grl = dn_inputs(p)
    qc, kc, vc, grc = dn_inputs(pc)
    y = _delta(qc, kc, vc, grc, ql, kl, vl, grl, p, qkv_dim, dn_norm_g[0], chunk)
    h = _matmul(y.reshape(t_lat, val_dim), dn_w_out[0].astype(BF16), out_dtype=F32, tm=1024,
                tn=512, modv=modv(0, 1), resid=h, name="dn_out_proj")
    h = _ffn(h, modv(0, 2), norm_g[0, 2], *w_ffn, 0, 1)

    h = _ffn(h, modv(1, 0), norm_g[1, 0], *w_ffn, 1, 0)
    gdim = d // N_FGROUPS
    cos_c, sin_c = _dft_tables(gdim)
    w_cs = _chan_dft_weights(jnp.stack([cos_c, sin_c]), fn_w_out[0].astype(BF16))
    ab = _matmul(h, w_cs, out_dtype=BF16, tm=1024, tn=1024, modv=modv(1, 1), norm_g=norm_g[1, 1],
                 name="fourier_in").reshape(bsz, seq, 2 * d)
    cos_l, sin_l = _dft_tables(seq // 2)
    ab_even, ab_odd = _dft_fold(ab)
    h = _seq_dft(cos_l, sin_l, ab_even, ab_odd, modv(1, 1), h.reshape(bsz, seq, d),
                 1.0 / math.sqrt(seq * gdim)).reshape(t_lat, d)
    h = _ffn(h, modv(1, 2), norm_g[1, 2], *w_ffn, 1, 1, final_g=final_norm_g)
    return h.reshape(bsz, seq, d)
```
